```python
import math
import jax, jax.numpy as jnp
from jax import lax
import numpy as np


D_MODEL = 1024
BATCH = 8
SEQ = 4096
DEPTH = 4

GRID_W = 64
CTX_LEN = 256
N_MIXERS = 2
EPS = 1e-6

SSD_EXPAND = 2
D_INNER = SSD_EXPAND * D_MODEL
HEAD_DIM = 64
N_SSD_HEADS = D_INNER // HEAD_DIM
N_SSD_GROUPS = 8
HEADS_PER_GROUP = N_SSD_HEADS // N_SSD_GROUPS
D_STATE = 128
SSD_CONV = 5
SSD_CHUNK = 128
D_XBC = D_INNER + 2 * N_SSD_GROUPS * D_STATE
D_IN_PROJ = D_INNER + D_XBC + 2 * N_SSD_HEADS

CONV_WIDTH = 31

N_EXPERTS = 32
TOP_K = 4
D_FF = D_MODEL
SWIGLU_ALPHA = 1.702
SWIGLU_LIMIT = 7.0
MOE_BLOCK = 128

N_SSD_LAYERS = (DEPTH + 1) // 2
N_CONV_LAYERS = DEPTH // 2

kernel_name = "hybrid_ssd_conformer_moe_dit"


def rms_norm(x, g):
    xf = x.astype(jnp.float32)
    y = xf * lax.rsqrt(jnp.mean(xf * xf, axis=-1, keepdims=True) + EPS)
    return (y * g.astype(jnp.float32)).astype(x.dtype)


def layer_norm(x, g, b):
    xf = x.astype(jnp.float32)
    mu = jnp.mean(xf, axis=-1, keepdims=True)
    xc = xf - mu
    y = xc * lax.rsqrt(jnp.mean(xc * xc, axis=-1, keepdims=True) + EPS)
    return (y * g.astype(jnp.float32) + b.astype(jnp.float32)).astype(x.dtype)


def modulate(h, shift, scale):
    return h * (1.0 + scale) + shift


def depthwise_conv(u, w, b):
    pad = (w.shape[0] - 1) // 2
    y = lax.conv_general_dilated(u, w[:, None, :].astype(u.dtype), window_strides=(1,),
                                 padding=[(pad, pad)], dimension_numbers=('NWC', 'WIO', 'NWC'),
                                 feature_group_count=u.shape[-1])
    return y + b


def ssd_scan(x, dt, a, b_mat, c_mat, init_state):
    bsz, seqlen = x.shape[:2]
    nc = seqlen // SSD_CHUNK
    shp = (bsz, nc, SSD_CHUNK, N_SSD_GROUPS)
    xdt = (x.astype(jnp.float32) * dt[..., None]).reshape(*shp, HEADS_PER_GROUP, HEAD_DIM)
    a_cum = jnp.cumsum((dt * a).reshape(*shp, HEADS_PER_GROUP), axis=2)
    bm = b_mat.astype(jnp.float32).reshape(*shp, D_STATE)
    cm = c_mat.astype(jnp.float32).reshape(*shp, D_STATE)
    seg = a_cum[:, :, :, None] - a_cum[:, :, None, :]
    lower = jnp.tril(jnp.ones((SSD_CHUNK, SSD_CHUNK), dtype=bool))[:, :, None, None]
    decay = jnp.exp(jnp.where(lower, seg, -jnp.inf))
    scores = jnp.einsum('bcign,bcjgn->bcijg', cm, bm)
    y_diag = jnp.einsum('bcijgr,bcjgrp->bcigrp', scores[..., None] * decay, xdt)
    decay_to_end = jnp.exp(a_cum[:, :, -1:] - a_cum)
    states = jnp.einsum('bcjgn,bcjgrp->bcgrpn', bm, xdt * decay_to_end[..., None])
    chunk_decay = jnp.exp(a_cum[:, :, -1])

    def step(s, inp):
        st, dc = inp
        return s * dc[..., None, None] + st, s

    final, prev = lax.scan(step, init_state,
                           (jnp.moveaxis(states, 1, 0), jnp.moveaxis(chunk_decay, 1, 0)))
    prev = jnp.moveaxis(prev, 0, 1)
    y_off = jnp.einsum('bcign,bcgrpn->bcigrp', cm, prev) * jnp.exp(a_cum)[..., None]
    y = (y_diag + y_off).reshape(bsz, seqlen, N_SSD_HEADS, HEAD_DIM)
    return y, final


def ssd_mixer(h_ctx, h_lat, w_in, w_conv, b_conv, dt_bias, a_log, d_skip, norm_g, w_out, ctx_out):
    a = -jnp.exp(a_log.astype(jnp.float32))

    def project(h):
        bsz, seqlen = h.shape[:2]
        zxbcdt = h @ w_in
        z = zxbcdt[..., :D_INNER]
        xbc = jax.nn.silu(depthwise_conv(zxbcdt[..., D_INNER:D_INNER + D_XBC], w_conv, b_conv))
        dt_raw = zxbcdt[..., D_INNER + D_XBC:].reshape(bsz, seqlen, 2, N_SSD_HEADS)
        dt = jax.nn.softplus(dt_raw.astype(jnp.float32) + dt_bias.astype(jnp.float32))
        xs = xbc[..., :D_INNER].reshape(bsz, seqlen, N_SSD_HEADS, HEAD_DIM)
        bm = xbc[..., D_INNER:D_INNER + N_SSD_GROUPS * D_STATE].reshape(bsz, seqlen, N_SSD_GROUPS, D_STATE)
        cm = xbc[..., D_INNER + N_SSD_GROUPS * D_STATE:].reshape(bsz, seqlen, N_SSD_GROUPS, D_STATE)
        return z, xs, bm, cm, dt

    def bidir(xs, bm, cm, dt, init_f, init_b):
        flip = lambda t: jnp.flip(t, axis=1)
        y_f, s_f = ssd_scan(xs, dt[:, :, 0], a[0], bm, cm, init_f)
        y_b, s_b = ssd_scan(flip(xs), flip(dt[:, :, 1]), a[1], flip(bm), flip(cm), init_b)
        y = y_f + flip(y_b) + d_skip.astype(jnp.float32)[:, None] * xs.astype(jnp.float32)
        return y, s_f, s_b

    def finish(y, z):
        bsz, seqlen = y.shape[:2]
        y = y.reshape(bsz, seqlen, D_INNER) * jax.nn.silu(z.astype(jnp.float32))
        y = rms_norm(y.reshape(bsz, seqlen, N_SSD_GROUPS, D_INNER // N_SSD_GROUPS),
                     norm_g.reshape(N_SSD_GROUPS, D_INNER // N_SSD_GROUPS))
        return y.reshape(bsz, seqlen, D_INNER).astype(z.dtype) @ w_out

    z_c, x_c, b_c, c_c, dt_c = project(h_ctx)
    zero = jnp.zeros((h_ctx.shape[0], N_SSD_GROUPS, HEADS_PER_GROUP, HEAD_DIM, D_STATE), jnp.float32)
    y_c, s_f, s_b = bidir(x_c, b_c, c_c, dt_c, zero, zero)
    z_l, x_l, b_l, c_l, dt_l = project(h_lat)
    y_l, _, _ = bidir(x_l, b_l, c_l, dt_l, s_f, s_b)
    out_ctx = finish(y_c, z_c) if ctx_out else None
    return out_ctx, finish(y_l, z_l)


def conformer_conv(h, w_pw1, b_pw1, w_dw, b_dw, ln_g, ln_b, w_pw2, b_pw2, rows):
    bsz, seqlen = h.shape[:2]
    u = h @ w_pw1 + b_pw1
    u = u[..., :D_MODEL] * jax.nn.sigmoid(u[..., D_MODEL:])
    if rows is not None:
        u = u.reshape(bsz * rows, GRID_W, D_MODEL)
    u = depthwise_conv(u, w_dw, b_dw).reshape(bsz, seqlen, D_MODEL)
    u = jax.nn.silu(layer_norm(u, ln_g, ln_b))
    return u @ w_pw2 + b_pw2


def moe_ffn(h, w_router, b_router, w_gu, b_gu, w_down, b_down):
    n_tok = h.shape[0]
    logits = (h @ w_router).astype(jnp.float32) + b_router.astype(jnp.float32)
    top_logit, top_idx = lax.top_k(logits, TOP_K)
    gates = jax.nn.softmax(top_logit, axis=-1).astype(h.dtype)
    n_assign = n_tok * TOP_K
    expert_flat = top_idx.reshape(-1)
    token_flat = jnp.arange(n_assign, dtype=jnp.int32) // TOP_K
    order = jnp.argsort(expert_flat)
    expert_sorted = expert_flat[order]
    token_sorted = token_flat[order]
    gate_sorted = gates.reshape(-1)[order]
    counts = jnp.bincount(expert_flat, length=N_EXPERTS)
    padded = (counts + MOE_BLOCK - 1) // MOE_BLOCK * MOE_BLOCK
    pad_end = jnp.cumsum(padded)
    pad_start = pad_end - padded
    unpad_start = jnp.cumsum(counts) - counts
    rank = jnp.arange(n_assign, dtype=jnp.int32) - unpad_start[expert_sorted]
    slot = pad_start[expert_sorted] + rank
    n_blocks = -(-n_assign // MOE_BLOCK) + N_EXPERTS
    slot_token = jnp.zeros((n_blocks * MOE_BLOCK,), jnp.int32).at[slot].set(token_sorted)
    block_start = jnp.arange(n_blocks, dtype=jnp.int32) * MOE_BLOCK
    block_expert = jnp.minimum(jnp.searchsorted(pad_end, block_start, side='right'), N_EXPERTS - 1)
    x_blocks = h[slot_token].reshape(n_blocks, MOE_BLOCK, D_MODEL)

    def expert_block(args):
        xb, e = args
        gu = xb @ w_gu[e] + b_gu[e]
        gate = jnp.minimum(gu[:, :D_FF], SWIGLU_LIMIT)
        up = jnp.clip(gu[:, D_FF:], -SWIGLU_LIMIT, SWIGLU_LIMIT)
        glu = gate * jax.nn.sigmoid(gate * SWIGLU_ALPHA)
        return ((up + 1.0) * glu) @ w_down[e] + b_down[e]

    y_blocks = lax.map(expert_block, (x_blocks, block_expert))
    y_sorted = y_blocks.reshape(n_blocks * MOE_BLOCK, D_MODEL)[slot]
    return jax.ops.segment_sum(y_sorted * gate_sorted[:, None], token_sorted, num_segments=n_tok)


def setup_inputs(seed: int = 0) -> dict:
    key = jax.random.key(seed)
    ks = jax.random.split(key, 40)
    f32 = jnp.float32
    nrm = lambda k, shape, s: jax.random.normal(k, shape, f32) * s
    dt0 = jnp.exp(jax.random.uniform(ks[9], (N_SSD_LAYERS, 2, N_SSD_HEADS), f32,
                                     minval=math.log(1e-3), maxval=math.log(1e-1)))
    return {
        'x': nrm(ks[0], (BATCH, SEQ, D_MODEL), 1.0),
        'c': nrm(ks[1], (BATCH, D_MODEL), 1.0),
        'ctx': nrm(ks[2], (BATCH, CTX_LEN, D_MODEL), 1.0),
        'c_ctx': nrm(ks[3], (D_MODEL,), 1.0),
        'w_ada': nrm(ks[4], (DEPTH, D_MODEL, 6 * D_MODEL), 0.5 * D_MODEL ** -0.5),
        'b_ada': nrm(ks[5], (DEPTH, 6 * D_MODEL), 0.02),
        'norm1_g': 1.0 + nrm(ks[6], (DEPTH, D_MODEL), 0.02),
        'norm2_g': 1.0 + nrm(ks[7], (DEPTH, D_MODEL), 0.02),
        'ssd_w_in': nrm(ks[8], (N_SSD_LAYERS, D_MODEL, D_IN_PROJ), D_MODEL ** -0.5),
        'ssd_w_conv': nrm(ks[10], (N_SSD_LAYERS, SSD_CONV, D_XBC), SSD_CONV ** -0.5),
        'ssd_b_conv': nrm(ks[11], (N_SSD_LAYERS, D_XBC), 0.02),
        'ssd_dt_bias': dt0 + jnp.log(-jnp.expm1(-dt0)),
        'ssd_a_log': jnp.log(jax.random.uniform(ks[12], (N_SSD_LAYERS, 2, N_SSD_HEADS), f32, minval=1.0, maxval=16.0)),
        'ssd_d': 1.0 + nrm(ks[13], (N_SSD_LAYERS, N_SSD_HEADS), 0.1),
        'ssd_norm_g': 1.0 + nrm(ks[14], (N_SSD_LAYERS, D_INNER), 0.02),
        'ssd_w_out': nrm(ks[15], (N_SSD_LAYERS, D_INNER, D_MODEL), D_INNER ** -0.5),
        'cv_w_pw1': nrm(ks[16], (N_CONV_LAYERS, D_MODEL, 2 * D_MODEL), D_MODEL ** -0.5),
        'cv_b_pw1': nrm(ks[17], (N_CONV_LAYERS, 2 * D_MODEL), 0.02),
        'cv_w_dw': nrm(ks[18], (N_CONV_LAYERS, CONV_WIDTH, D_MODEL), CONV_WIDTH ** -0.5),
        'cv_b_dw': nrm(ks[19], (N_CONV_LAYERS, D_MODEL), 0.02),
        'cv_ln_g': 1.0 + nrm(ks[20], (N_CONV_LAYERS, D_MODEL), 0.02),
        'cv_ln_b': nrm(ks[21], (N_CONV_LAYERS, D_MODEL), 0.02),
        'cv_w_pw2': nrm(ks[22], (N_CONV_LAYERS, D_MODEL, D_MODEL), D_MODEL ** -0.5),
        'cv_b_pw2': nrm(ks[23], (N_CONV_LAYERS, D_MODEL), 0.02),
        'moe_w_router': nrm(ks[24], (DEPTH, D_MODEL, N_EXPERTS), D_MODEL ** -0.5),
        'moe_b_router': nrm(ks[25], (DEPTH, N_EXPERTS), 0.01),
        'moe_w_gu': nrm(ks[26], (DEPTH, N_EXPERTS, D_MODEL, 2 * D_FF), D_MODEL ** -0.5),
        'moe_b_gu': nrm(ks[27], (DEPTH, N_EXPERTS, 2 * D_FF), 0.02),
        'moe_w_down': nrm(ks[28], (DEPTH, N_EXPERTS, D_FF, D_MODEL), D_FF ** -0.5),
        'moe_b_down': nrm(ks[29], (DEPTH, N_EXPERTS, D_MODEL), 0.02),
        'final_g': 1.0 + nrm(ks[30], (D_MODEL,), 0.02),
    }


def reference(x, c, ctx, c_ctx, w_ada, b_ada, norm1_g, norm2_g,
              ssd_w_in, ssd_w_conv, ssd_b_conv, ssd_dt_bias, ssd_a_log, ssd_d, ssd_norm_g, ssd_w_out,
              cv_w_pw1, cv_b_pw1, cv_w_dw, cv_b_dw, cv_ln_g, cv_ln_b, cv_w_pw2, cv_b_pw2,
              moe_w_router, moe_b_router, moe_w_gu, moe_b_gu, moe_w_down, moe_b_down, final_g):
    rows = x.shape[1] // GRID_W
    for i in range(DEPTH):
        j = i // N_MIXERS
        use_ssd = (i % N_MIXERS) == 0
        need_ctx = i < DEPTH - 1
        mod_l = jnp.split((jax.nn.silu(c) @ w_ada[i] + b_ada[i])[:, None, :], 6, axis=-1)
        h_lat = modulate(rms_norm(x, norm1_g[i]), mod_l[0], mod_l[1])
        if use_ssd or need_ctx:
            mod_c = jnp.split(jax.nn.silu(c_ctx) @ w_ada[i] + b_ada[i], 6, axis=-1)
            h_ctx = modulate(rms_norm(ctx, norm1_g[i]), mod_c[0], mod_c[1])
        if use_ssd:
            y_ctx, y_lat = ssd_mixer(h_ctx, h_lat, ssd_w_in[j], ssd_w_conv[j], ssd_b_conv[j], ssd_dt_bias[j],
                                     ssd_a_log[j], ssd_d[j], ssd_norm_g[j], ssd_w_out[j], need_ctx)
        else:
            y_lat = conformer_conv(h_lat, cv_w_pw1[j], cv_b_pw1[j], cv_w_dw[j], cv_b_dw[j],
                                   cv_ln_g[j], cv_ln_b[j], cv_w_pw2[j], cv_b_pw2[j], rows)
            y_ctx = None
            if need_ctx:
                y_ctx = conformer_conv(h_ctx, cv_w_pw1[j], cv_b_pw1[j], cv_w_dw[j], cv_b_dw[j],
                                       cv_ln_g[j], cv_ln_b[j], cv_w_pw2[j], cv_b_pw2[j], None)
        x = x + mod_l[2] * y_lat
        h_lat = modulate(rms_norm(x, norm2_g[i]), mod_l[3], mod_l[4]).reshape(-1, D_MODEL)
        if need_ctx:
            ctx = ctx + mod_c[2] * y_ctx
            h_ctx = modulate(rms_norm(ctx, norm2_g[i]), mod_c[3], mod_c[4]).reshape(-1, D_MODEL)
            n_ctx_tok = h_ctx.shape[0]
            f = moe_ffn(jnp.concatenate([h_ctx, h_lat], axis=0), moe_w_router[i], moe_b_router[i],
                        moe_w_gu[i], moe_b_gu[i], moe_w_down[i], moe_b_down[i])
            ctx = ctx + mod_c[5] * f[:n_ctx_tok].reshape(ctx.shape)
            f_lat = f[n_ctx_tok:]
        else:
            f_lat = moe_ffn(h_lat, moe_w_router[i], moe_b_router[i], moe_w_gu[i], moe_b_gu[i],
                            moe_w_down[i], moe_b_down[i])
        x = x + mod_l[5] * f_lat.reshape(x.shape)
    return rms_norm(x, final_g)
```

```python
import functools

import jax
import jax.numpy as jnp
from jax import lax
from jax.experimental import pallas as pl
from jax.experimental.pallas import tpu as pltpu

F32 = jnp.float32
BF16 = jnp.bfloat16
MXU_DTYPE = BF16
ACT_DTYPE = BF16

EPS = 1e-6
GRID_W = 64
N_MIXERS = 2
HEAD_DIM = 64
N_SSD_HEADS = 32
N_SSD_GROUPS = 8
HEADS_PER_GROUP = N_SSD_HEADS // N_SSD_GROUPS
D_STATE = 128
SSD_CHUNK = 128
GROUP_W = HEADS_PER_GROUP * HEAD_DIM
D_INNER = N_SSD_HEADS * HEAD_DIM
D_BC = N_SSD_GROUPS * D_STATE
D_XBC = D_INNER + 2 * D_BC
TOP_K = 4
SWIGLU_ALPHA = 1.702
SWIGLU_LIMIT = 7.0

TILE = 256
CONV_PAD = 16
SSD_HALO = 8
N_CHUNK = 512
MOE_TM = 512
VMEM_MB = 1024 * 1024


def _cparams(n_axes, vmem_mb):
    return pltpu.CompilerParams(dimension_semantics=("arbitrary",) * n_axes,
                                vmem_limit_bytes=vmem_mb * VMEM_MB)


def _full_spec(shape):
    zeros = (0,) * len(shape)
    return pl.BlockSpec(shape, lambda *_: zeros)


def _dot(a, b):
    return jnp.dot(a.astype(MXU_DTYPE), b.astype(MXU_DTYPE), preferred_element_type=F32)


def _dot_nt(a, b):
    return lax.dot_general(a.astype(MXU_DTYPE), b.astype(MXU_DTYPE), (((1,), (1,)), ((), ())),
                           preferred_element_type=F32)


def _dot_tn(a, b):
    return lax.dot_general(a.astype(MXU_DTYPE), b.astype(MXU_DTYPE), (((0,), (0,)), ((), ())),
                           preferred_element_type=F32)


def _split3(v):
    hi = v.astype(BF16)
    r1 = v - hi.astype(F32)
    mid = r1.astype(BF16)
    lo = (r1 - mid.astype(F32)).astype(BF16)
    return hi, mid, lo


def _dot01_left(m01, v):
    hi, mid, lo = _split3(v)
    m = m01.astype(BF16)
    d = lambda p: jnp.dot(m, p, preferred_element_type=F32)
    return d(hi) + d(mid) + d(lo)


def _dot01_right(v, m01):
    hi, mid, lo = _split3(v)
    m = m01.astype(BF16)
    d = lambda p: jnp.dot(p, m, preferred_element_type=F32)
    return d(hi) + d(mid) + d(lo)


def _dot_nt_precise(a, b):
    ah = a.astype(BF16)
    al = (a - ah.astype(F32)).astype(BF16)
    bh = b.astype(BF16)
    bl = (b - bh.astype(F32)).astype(BF16)
    d = lambda p, q: lax.dot_general(p, q, (((1,), (1,)), ((), ())), preferred_element_type=F32)
    return d(ah, bh) + d(ah, bl) + d(al, bh)


def _sigmoid(v):
    return 1.0 / (1.0 + jnp.exp(-v))


def _silu(v):
    return v * _sigmoid(v)


def _softplus(v):
    return jnp.maximum(v, 0.0) + jnp.log(1.0 + jnp.exp(-jnp.abs(v)))


def _rms_mod(x, g, shift, scale):
    ms = jnp.mean(x * x, axis=-1, keepdims=True)
    y = x * lax.rsqrt(ms + EPS) * g
    return y * (1.0 + scale) + shift


def _ada_kernel(c_ref, w_ref, b_ref, o_ref):
    o_ref[...] = _dot(_silu(c_ref[...]), w_ref[...]) + b_ref[...]


def _ada_mods(c_all, w_ada, b_ada):
    depth, d, d6 = w_ada.shape
    n_mod = d6 // d
    rows = c_all.shape[0]
    out = pl.pallas_call(
        _ada_kernel,
        grid=(depth, n_mod),
        in_specs=[
            pl.BlockSpec((rows, d), lambda i, n: (0, 0)),
            pl.BlockSpec((None, d, d), lambda i, n: (i, 0, n)),
            pl.BlockSpec((None, None, 1, d), lambda i, n: (i, n, 0, 0)),
        ],
        out_specs=pl.BlockSpec((None, None, rows, d), lambda i, n: (i, n, 0, 0)),
        out_shape=jax.ShapeDtypeStruct((depth, n_mod, rows, d), F32),
        compiler_params=_cparams(2, 32),
        name="ada_mods",
    )(c_all, w_ada, b_ada.reshape(depth, n_mod, 1, d))
    return out.transpose(0, 2, 1, 3)


def _moe_pre(x_new, mod, g2, wr_t, b_r, h2_ref, topi_ref, gate_ref):
    h2 = _rms_mod(x_new, g2, mod[3:4], mod[4:5])
    h2_ref[...] = h2.astype(h2_ref.dtype)
    logits = _dot_nt_precise(wr_t, h2) + b_r
    n_exp = logits.shape[0]
    eidx = lax.broadcasted_iota(jnp.int32, logits.shape, 0)
    vals, idxs = [], []
    cur = logits
    for _ in range(TOP_K):
        m = jnp.max(cur, axis=0, keepdims=True)
        idx = jnp.min(jnp.where(cur == m, eidx, n_exp), axis=0, keepdims=True)
        vals.append(m)
        idxs.append(idx)
        cur = jnp.where(eidx == idx, -jnp.inf, cur)
    es = [jnp.exp(v - vals[0]) for v in vals]
    tot = es[0] + es[1] + es[2] + es[3]
    for k in range(TOP_K):
        topi_ref[k:k + 1, :] = idxs[k]
        gate_ref[k:k + 1, :] = es[k] / tot


def _moe_pre_specs(d, n_exp):
    in_specs = [_full_spec((1, d)), _full_spec((n_exp, d)), _full_spec((n_exp, 1))]
    out_specs = [
        pl.BlockSpec((None, TILE, d), lambda b, t: (b, t, 0)),
        pl.BlockSpec((None, TILE, d), lambda b, t: (b, t, 0)),
        pl.BlockSpec((None, TOP_K, TILE), lambda b, t: (b, 0, t)),
        pl.BlockSpec((None, TOP_K, TILE), lambda b, t: (b, 0, t)),
    ]
    return in_specs, out_specs


def _moe_pre_shapes(bsz, s_out, d):
    return [
        jax.ShapeDtypeStruct((bsz, s_out, d), F32),
        jax.ShapeDtypeStruct((bsz, s_out, d), ACT_DTYPE),
        jax.ShapeDtypeStruct((bsz, TOP_K, s_out), jnp.int32),
        jax.ShapeDtypeStruct((bsz, TOP_K, s_out), F32),
    ]


def _inproj_kernel(x_ref, mod_ref, g_ref, wz_ref, wx_ref, wdt_ref, wdt_t_ref, dtb_ref, dtb_t_ref,
                   a_ref, a_t_ref, z_ref, xbc_ref, dt_ref, acum_ref, acum_t_ref):
    mod = mod_ref[...]
    h = _rms_mod(x_ref[...], g_ref[...], mod[0:1], mod[1:2]).astype(MXU_DTYPE)
    for g in range(N_SSD_GROUPS):
        z_ref[g] = _dot(h, wz_ref[:, g * GROUP_W:(g + 1) * GROUP_W]).astype(z_ref.dtype)
    for n in range(0, D_XBC, N_CHUNK):
        xbc_ref[:, n:n + N_CHUNK] = _dot(h, wx_ref[:, n:n + N_CHUNK]).astype(xbc_ref.dtype)
    dt = _softplus(_dot(h, wdt_ref[...]) + dtb_ref[...])
    dt_t = _softplus(_dot_nt(wdt_t_ref[...], h) + dtb_t_ref[...])
    dt_ref[...] = dt
    dta = dt * a_ref[...]
    dta_t = dt_t * a_t_ref[...]
    ii = lax.broadcasted_iota(jnp.int32, (TILE, TILE), 0)
    jj = lax.broadcasted_iota(jnp.int32, (TILE, TILE), 1)
    same = (ii // SSD_CHUNK) == (jj // SSD_CHUNK)
    lower = jnp.where(same, jnp.where(jj <= ii, 1.0, 0.0), 0.0)
    upper = jnp.where(same, jnp.where(jj >= ii, 1.0, 0.0), 0.0)
    col = lax.broadcasted_iota(jnp.int32, dta.shape, 1)
    acum_ref[...] = jnp.where(col < N_SSD_HEADS, _dot01_left(lower, dta), _dot01_left(upper, dta))
    row = lax.broadcasted_iota(jnp.int32, dta_t.shape, 0)
    acum_t_ref[...] = jnp.where(row < N_SSD_HEADS, _dot01_right(dta_t, upper), _dot01_right(dta_t, lower))


def _ssd_inproj(xs, mod_i, g1, w_in, dt_bias, a_log):
    bsz, s, d = xs.shape
    nt = s // TILE
    ctx_row = bsz
    wz = w_in[:, :D_INNER].astype(MXU_DTYPE)
    wx = w_in[:, D_INNER:D_INNER + D_XBC].astype(MXU_DTYPE)
    wdt = w_in[:, D_INNER + D_XBC:].astype(MXU_DTYPE)
    n_dt = 2 * N_SSD_HEADS
    a = -jnp.exp(a_log.astype(F32)).reshape(1, n_dt)
    dtb = dt_bias.astype(F32).reshape(1, n_dt)
    tok = lambda w: pl.BlockSpec((None, TILE, w), lambda b, t: (b, t, 0))
    return pl.pallas_call(
        _inproj_kernel,
        grid=(bsz, nt),
        in_specs=[
            tok(d),
            pl.BlockSpec((None, 6, d), lambda b, t: (jnp.where(t == 0, ctx_row, b), 0, 0)),
            _full_spec((1, d)),
            _full_spec((d, D_INNER)), _full_spec((d, D_XBC)), _full_spec((d, n_dt)), _full_spec((n_dt, d)),
            _full_spec((1, n_dt)), _full_spec((n_dt, 1)), _full_spec((1, n_dt)), _full_spec((n_dt, 1)),
        ],
        out_specs=[
            pl.BlockSpec((None, N_SSD_GROUPS, TILE, GROUP_W), lambda b, t: (b, 0, t, 0)),
            tok(D_XBC), tok(n_dt), tok(n_dt),
            pl.BlockSpec((None, n_dt, TILE), lambda b, t: (b, 0, t)),
        ],
        out_shape=[
            jax.ShapeDtypeStruct((bsz, N_SSD_GROUPS, s, GROUP_W), ACT_DTYPE),
            jax.ShapeDtypeStruct((bsz, s, D_XBC), ACT_DTYPE),
            jax.ShapeDtypeStruct((bsz, s, n_dt), F32),
            jax.ShapeDtypeStruct((bsz, s, n_dt), F32),
            jax.ShapeDtypeStruct((bsz, n_dt, s), F32),
        ],
        compiler_params=_cparams(2, 56),
        name="ssd_inproj",
    )(xs, mod_i, g1.reshape(1, d), wz, wx, wdt, wdt.T, dtb, dtb.T, a, a.T)


def _ssd_conv_kernel(nt, main_ref, prev_ref, next_ref, w_ref, b_ref, xs_ref, bm_ref, cm_ref, buf):
    t = pl.program_id(1)
    width = w_ref.shape[0]
    half = (width - 1) // 2
    prev_ok = jnp.where(t >= 2, 1.0, 0.0)
    next_ok = jnp.where(jnp.logical_and(t >= 1, t < nt - 1), 1.0, 0.0)
    buf[0:SSD_HALO, :] = prev_ref[...].astype(F32) * prev_ok
    buf[SSD_HALO:SSD_HALO + TILE, :] = main_ref[...].astype(F32)
    buf[SSD_HALO + TILE:, :] = next_ref[...].astype(F32) * next_ok
    rows = 64
    for r0 in range(0, TILE, rows):
        for n in range(0, D_XBC, N_CHUNK):
            acc = jnp.zeros((rows, N_CHUNK), F32) + b_ref[:, n:n + N_CHUNK]
            for k in range(width):
                off = SSD_HALO + r0 + k - half
                acc = acc + w_ref[k:k + 1, n:n + N_CHUNK] * buf[off:off + rows, n:n + N_CHUNK]
            out = _silu(acc)
            for c0 in range(n, n + N_CHUNK, D_STATE):
                piece = out[:, c0 - n:c0 - n + D_STATE]
                if c0 < D_INNER:
                    g, o = divmod(c0, GROUP_W)
                    xs_ref[g, r0:r0 + rows, o:o + D_STATE] = piece.astype(xs_ref.dtype)
                elif c0 < D_INNER + D_BC:
                    bm_ref[(c0 - D_INNER) // D_STATE, r0:r0 + rows, :] = piece.astype(bm_ref.dtype)
                else:
                    cm_ref[(c0 - D_INNER - D_BC) // D_STATE, r0:r0 + rows, :] = piece.astype(cm_ref.dtype)


def _ssd_conv(xbc, w_conv, b_conv):
    bsz, s, c = xbc.shape
    nt = s // TILE
    per_tile = TILE // SSD_HALO
    last_halo = s // SSD_HALO - 1
    grp = lambda w: pl.BlockSpec((None, N_SSD_GROUPS, TILE, w), lambda b, t: (b, 0, t, 0))
    return pl.pallas_call(
        functools.partial(_ssd_conv_kernel, nt),
        grid=(bsz, nt),
        in_specs=[
            pl.BlockSpec((None, TILE, c), lambda b, t: (b, t, 0)),
            pl.BlockSpec((None, SSD_HALO, c), lambda b, t: (b, jnp.maximum(t * per_tile - 1, 0), 0)),
            pl.BlockSpec((None, SSD_HALO, c), lambda b, t: (b, jnp.minimum((t + 1) * per_tile, last_halo), 0)),
            _full_spec(w_conv.shape), _full_spec((1, c)),
        ],
        out_specs=[grp(GROUP_W), grp(D_STATE), grp(D_STATE)],
        out_shape=[
            jax.ShapeDtypeStruct((bsz, N_SSD_GROUPS, s, GROUP_W), ACT_DTYPE),
            jax.ShapeDtypeStruct((bsz, N_SSD_GROUPS, s, D_STATE), ACT_DTYPE),
            jax.ShapeDtypeStruct((bsz, N_SSD_GROUPS, s, D_STATE), ACT_DTYPE),
        ],
        scratch_shapes=[pltpu.VMEM((TILE + 2 * SSD_HALO, c), F32)],
        compiler_params=_cparams(2, 48),
        name="ssd_conv",
    )(xbc, xbc, xbc, w_conv.astype(F32), b_conv.astype(F32).reshape(1, c))


def _ssd_scan_kernel(xs_ref, bm_ref, cm_ref, dt_ref, acol_ref, arow_ref, y_ref, st_ref):
    d = pl.program_id(0)
    c = pl.program_id(2)

    @pl.when(c == 0)
    def _():
        st_ref[...] = jnp.zeros_like(st_ref)

    q = SSD_CHUNK
    ii = lax.broadcasted_iota(jnp.int32, (q, q), 0)
    jj = lax.broadcasted_iota(jnp.int32, (q, q), 1)
    mask = (ii - jj) * (1 - 2 * d) >= 0
    head_of_lane = lax.broadcasted_iota(jnp.int32, (1, GROUP_W), 1) // HEAD_DIM

    def expand(v):
        out = v[:, HEADS_PER_GROUP - 1:HEADS_PER_GROUP]
        for r in range(HEADS_PER_GROUP - 2, -1, -1):
            out = jnp.where(head_of_lane == r, v[:, r:r + 1], out)
        return out

    for g in range(N_SSD_GROUPS):
        hs = slice(g * HEADS_PER_GROUP, (g + 1) * HEADS_PER_GROUP)
        x = xs_ref[g].astype(F32)
        bm = bm_ref[g]
        cm = cm_ref[g]
        dtc = dt_ref[:, hs]
        acol = acol_ref[:, hs]
        arow = arow_ref[hs, :]
        a_tot = jnp.where(d == 0, acol[q - 1:q, :], acol[0:1, :])
        scores = _dot_nt(cm, bm)
        st = st_ref[g]
        xdt = x * expand(dtc)
        xdt_m = xdt.astype(MXU_DTYPE)
        y = _dot(cm, st) * expand(jnp.exp(acol))
        for r in range(HEADS_PER_GROUP):
            seg = acol[:, r:r + 1] - arow[r:r + 1, :]
            decay = jnp.where(mask, jnp.exp(seg), 0.0)
            yr = _dot(scores * decay, xdt_m)
            y = y + jnp.where(head_of_lane == r, yr, 0.0)
        y_ref[g] = y.astype(y_ref.dtype)
        xdte = xdt * expand(jnp.exp(a_tot - acol))
        st_ref[g] = st * expand(jnp.exp(a_tot)) + _dot_tn(bm, xdte)


def _ssd_scan(xs_g, bm_g, cm_g, dt_d, acum_d, acum_t_d, n_ctx_chunks):
    bsz, n_grp, s, _ = xs_g.shape
    nc = s // SSD_CHUNK
    last = nc - 1 + n_ctx_chunks

    def chunk(d, c):
        back = jnp.where(c < n_ctx_chunks, n_ctx_chunks - 1 - c, last - c)
        return jnp.where(d == 0, c, back)

    big = lambda w: pl.BlockSpec((None, n_grp, SSD_CHUNK, w), lambda d, b, c: (b, 0, chunk(d, c), 0))
    col = pl.BlockSpec((None, None, SSD_CHUNK, N_SSD_HEADS), lambda d, b, c: (d, b, chunk(d, c), 0))
    return pl.pallas_call(
        _ssd_scan_kernel,
        grid=(2, bsz, nc),
        in_specs=[
            big(GROUP_W), big(D_STATE), big(D_STATE), col, col,
            pl.BlockSpec((None, None, N_SSD_HEADS, SSD_CHUNK), lambda d, b, c: (d, b, 0, chunk(d, c))),
        ],
        out_specs=pl.BlockSpec((None, None, n_grp, SSD_CHUNK, GROUP_W),
                               lambda d, b, c: (d, b, 0, chunk(d, c), 0)),
        out_shape=jax.ShapeDtypeStruct((2, bsz, n_grp, s, GROUP_W), ACT_DTYPE),
        scratch_shapes=[pltpu.VMEM((n_grp, D_STATE, GROUP_W), F32)],
        compiler_params=_cparams(3, 32),
        name="ssd_scan",
    )(xs_g, bm_g, cm_g, dt_d, acum_d, acum_t_d)


def _ssd_out_kernel(x_ref, mod_ref, yf_ref, yb_ref, xs_ref, z_ref, dsk_ref, ng_ref, wo_ref,
                    g2_ref, wr_ref, br_ref, xo_ref, h2_ref, topi_ref, gate_ref, yn_scr):
    mod = mod_ref[...]
    for g in range(N_SSD_GROUPS):
        y = yf_ref[g].astype(F32) + yb_ref[g].astype(F32) + dsk_ref[g] * xs_ref[g].astype(F32)
        y = y * _silu(z_ref[g].astype(F32))
        ms = jnp.mean(y * y, axis=-1, keepdims=True)
        yn_scr[:, g * GROUP_W:(g + 1) * GROUP_W] = (y * lax.rsqrt(ms + EPS) * ng_ref[g]).astype(yn_scr.dtype)
    x_new = x_ref[...] + mod[2:3] * _dot(yn_scr[...], wo_ref[...])
    xo_ref[...] = x_new
    _moe_pre(x_new, mod, g2_ref[...], wr_ref[...], br_ref[...], h2_ref, topi_ref, gate_ref)


def _ssd_out(xs, mod_i, y, xs_g, z_g, d_skip, norm_g, w_out, g2, w_router, b_router):
    bsz, s, d = xs.shape
    nt = s // TILE
    ctx_row = bsz
    n_exp = w_router.shape[1]
    grp = lambda: pl.BlockSpec((None, N_SSD_GROUPS, TILE, GROUP_W), lambda b, t: (b, 0, t, 0))
    ydir = lambda dd: pl.BlockSpec((None, None, N_SSD_GROUPS, TILE, GROUP_W), lambda b, t: (dd, b, 0, t, 0))
    pre_in, pre_out = _moe_pre_specs(d, n_exp)
    dsk = jnp.repeat(d_skip.astype(F32), HEAD_DIM).reshape(N_SSD_GROUPS, 1, GROUP_W)
    return pl.pallas_call(
        _ssd_out_kernel,
        grid=(bsz, nt),
        in_specs=[
            pl.BlockSpec((None, TILE, d), lambda b, t: (b, t, 0)),
            pl.BlockSpec((None, 6, d), lambda b, t: (jnp.where(t == 0, ctx_row, b), 0, 0)),
            ydir(0), ydir(1), grp(), grp(),
            _full_spec((N_SSD_GROUPS, 1, GROUP_W)), _full_spec((N_SSD_GROUPS, 1, GROUP_W)),
            _full_spec((D_INNER, d)),
        ] + pre_in,
        out_specs=pre_out,
        out_shape=_moe_pre_shapes(bsz, s, d),
        scratch_shapes=[pltpu.VMEM((TILE, D_INNER), MXU_DTYPE)],
        compiler_params=_cparams(2, 48),
        name="ssd_out",
    )(xs, mod_i, y, y, xs_g, z_g, dsk, norm_g.astype(F32).reshape(N_SSD_GROUPS, 1, GROUP_W),
      w_out.astype(MXU_DTYPE), g2.reshape(1, d), w_router.T.astype(F32), b_router.astype(F32).reshape(n_exp, 1))


def _cv_kernel(t0, x_ref, mod_ref, g1_ref, w1_ref, b1_ref, wdw_ref, bdw_ref, lng_ref, lnb_ref, w2_ref, b2_ref,
               g2_ref, wr_ref, br_ref, xo_ref, h2_ref, topi_ref, gate_ref, pad_scr, cv_scr):
    tile = pl.program_id(1) + t0
    d = x_ref.shape[-1]
    x = x_ref[...]
    mod = mod_ref[...]
    h = _rms_mod(x, g1_ref[...], mod[0:1], mod[1:2]).astype(MXU_DTYPE)
    width = wdw_ref.shape[0]
    half = (width - 1) // 2
    seg = GRID_W
    n_seg = TILE // seg
    stride = seg + 2 * CONV_PAD
    joined = jnp.where(tile == 0, 1.0, 0.0)
    zeros = jnp.zeros((CONV_PAD, d), F32)
    pad_scr[0:CONV_PAD, :] = zeros
    pad_scr[n_seg * stride - CONV_PAD:n_seg * stride, :] = zeros
    for n in range(0, d, N_CHUNK):
        a = _dot(h, w1_ref[:, n:n + N_CHUNK]) + b1_ref[:, n:n + N_CHUNK]
        gate = _dot(h, w1_ref[:, d + n:d + n + N_CHUNK]) + b1_ref[:, d + n:d + n + N_CHUNK]
        u = a * _sigmoid(gate)
        for s_i in range(n_seg):
            base = s_i * stride
            pad_scr[base + CONV_PAD:base + CONV_PAD + seg, n:n + N_CHUNK] = u[s_i * seg:(s_i + 1) * seg]
            if s_i > 0:
                pad_scr[base:base + CONV_PAD, n:n + N_CHUNK] = u[s_i * seg - CONV_PAD:s_i * seg] * joined
            if s_i < n_seg - 1:
                pad_scr[base + CONV_PAD + seg:base + stride, n:n + N_CHUNK] = (
                    u[(s_i + 1) * seg:(s_i + 1) * seg + CONV_PAD] * joined)
    lanes = 256
    for s_i in range(n_seg):
        for n in range(0, d, lanes):
            acc = jnp.zeros((seg, lanes), F32) + bdw_ref[:, n:n + lanes]
            for k in range(width):
                off = s_i * stride + CONV_PAD + k - half
                acc = acc + wdw_ref[k:k + 1, n:n + lanes] * pad_scr[off:off + seg, n:n + lanes]
            cv_scr[s_i * seg:(s_i + 1) * seg, n:n + lanes] = acc
    cv = cv_scr[...]
    mu = jnp.mean(cv, axis=-1, keepdims=True)
    xc = cv - mu
    var = jnp.mean(xc * xc, axis=-1, keepdims=True)
    ln = xc * lax.rsqrt(var + EPS) * lng_ref[...] + lnb_ref[...]
    out = _dot(_silu(ln), w2_ref[...]) + b2_ref[...]
    x_new = x + mod[2:3] * out
    xo_ref[...] = x_new
    _moe_pre(x_new, mod, g2_ref[...], wr_ref[...], br_ref[...], h2_ref, topi_ref, gate_ref)


def _conformer(xs, mod_i, g1, w_pw1, b_pw1, w_dw, b_dw, ln_g, ln_b, w_pw2, b_pw2, g2, w_router, b_router,
               with_ctx):
    bsz, s, d = xs.shape
    t0 = 0 if with_ctx else 1
    nt = s // TILE - t0
    ctx_row = bsz
    n_exp = w_router.shape[1]
    pre_in, pre_out = _moe_pre_specs(d, n_exp)
    row = lambda v: v.astype(F32).reshape(1, -1)
    return pl.pallas_call(
        functools.partial(_cv_kernel, t0),
        grid=(bsz, nt),
        in_specs=[
            pl.BlockSpec((None, TILE, d), lambda b, t: (b, t + t0, 0)),
            pl.BlockSpec((None, 6, d), lambda b, t: (jnp.where(t + t0 == 0, ctx_row, b), 0, 0)),
            _full_spec((1, d)),
            _full_spec((d, 2 * d)), _full_spec((1, 2 * d)),
            _full_spec(w_dw.shape), _full_spec((1, d)), _full_spec((1, d)), _full_spec((1, d)),
            _full_spec((d, d)), _full_spec((1, d)),
        ] + pre_in,
        out_specs=pre_out,
        out_shape=_moe_pre_shapes(bsz, nt * TILE, d),
        scratch_shapes=[
            pltpu.VMEM((TILE + 2 * CONV_PAD * (TILE // GRID_W), d), F32),
            pltpu.VMEM((TILE, d), F32),
        ],
        compiler_params=_cparams(2, 48),
        name="conformer",
    )(xs, mod_i, row(g1), w_pw1.astype(MXU_DTYPE), row(b_pw1), w_dw.astype(F32), row(b_dw), row(ln_g), row(ln_b),
      w_pw2.astype(MXU_DTYPE), row(b_pw2), row(g2), w_router.T.astype(F32),
      b_router.astype(F32).reshape(n_exp, 1))


def _moe_kernel(be_ref, flag_ref, x_ref, wgu_ref, bgu_ref, wdn_ref, bdn_ref, y_ref, wgu_scr, wdn_scr, act_scr):
    i = pl.program_id(0)
    flags = flag_ref[i]
    d_ff = wdn_ref.shape[0]

    @pl.when((flags & 2) != 0)
    def _():
        for n in range(0, 2 * d_ff, N_CHUNK):
            wgu_scr[:, n:n + N_CHUNK] = wgu_ref[:, n:n + N_CHUNK].astype(wgu_scr.dtype)
        for n in range(0, wdn_ref.shape[1], N_CHUNK):
            wdn_scr[:, n:n + N_CHUNK] = wdn_ref[:, n:n + N_CHUNK].astype(wdn_scr.dtype)

    @pl.when((flags & 1) != 0)
    def _():
        x = x_ref[...]
        for n in range(0, d_ff, N_CHUNK):
            gate = _dot(x, wgu_scr[:, n:n + N_CHUNK]) + bgu_ref[:, n:n + N_CHUNK]
            up = _dot(x, wgu_scr[:, d_ff + n:d_ff + n + N_CHUNK]) + bgu_ref[:, d_ff + n:d_ff + n + N_CHUNK]
            gate = jnp.minimum(gate, SWIGLU_LIMIT)
            up = jnp.clip(up, -SWIGLU_LIMIT, SWIGLU_LIMIT)
            glu = gate * _sigmoid(gate * SWIGLU_ALPHA)
            act_scr[:, n:n + N_CHUNK] = ((up + 1.0) * glu).astype(act_scr.dtype)
        act = act_scr[...]
        for n in range(0, y_ref.shape[1], N_CHUNK):
            y_ref[:, n:n + N_CHUNK] = (_dot(act, wdn_scr[:, n:n + N_CHUNK])
                                       + bdn_ref[:, n:n + N_CHUNK]).astype(y_ref.dtype)

    @pl.when((flags & 1) == 0)
    def _():
        y_ref[...] = jnp.zeros_like(y_ref)


def _moe_experts(xg, block_expert, block_flags, w_gu, b_gu, w_down, b_down):
    n_rows, d = xg.shape
    n_exp, _, d_gu = w_gu.shape
    d_ff = w_down.shape[1]
    nb = n_rows // MOE_TM
    grid_spec = pltpu.PrefetchScalarGridSpec(
        num_scalar_prefetch=2,
        grid=(nb,),
        in_specs=[
            pl.BlockSpec((MOE_TM, d), lambda i, be, fl: (i, 0)),
            pl.BlockSpec((None, d, d_gu), lambda i, be, fl: (be[i], 0, 0)),
            pl.BlockSpec((None, 1, d_gu), lambda i, be, fl: (be[i], 0, 0)),
            pl.BlockSpec((None, d_ff, d), lambda i, be, fl: (be[i], 0, 0)),
            pl.BlockSpec((None, 1, d), lambda i, be, fl: (be[i], 0, 0)),
        ],
        out_specs=pl.BlockSpec((MOE_TM, d), lambda i, be, fl: (i, 0)),
        scratch_shapes=[
            pltpu.VMEM((d, d_gu), MXU_DTYPE),
            pltpu.VMEM((d_ff, d), MXU_DTYPE),
            pltpu.VMEM((MOE_TM, d_ff), MXU_DTYPE),
        ],
    )
    return pl.pallas_call(
        _moe_kernel,
        grid_spec=grid_spec,
        out_shape=jax.ShapeDtypeStruct((n_rows, d), ACT_DTYPE),
        compiler_params=_cparams(1, 56),
        name="moe_experts",
    )(block_expert, block_flags, xg, w_gu, b_gu.reshape(n_exp, 1, d_gu), w_down, b_down.reshape(n_exp, 1, d))


def _route(topi, n_exp):
    n_tok = topi.shape[0]
    n_assign = n_tok * TOP_K
    e_flat = topi.reshape(-1)
    order = jnp.argsort(e_flat, stable=True).astype(jnp.int32)
    e_sorted = e_flat[order]
    counts = jnp.sum((e_flat[:, None] == jnp.arange(n_exp, dtype=jnp.int32)[None, :]).astype(jnp.int32), axis=0)
    padded = (counts + MOE_TM - 1) // MOE_TM * MOE_TM
    pad_end = jnp.cumsum(padded)
    pad_start = pad_end - padded
    unpad_start = jnp.cumsum(counts) - counts
    rank = jnp.arange(n_assign, dtype=jnp.int32) - unpad_start[e_sorted]
    slot_sorted = (pad_start[e_sorted] + rank).astype(jnp.int32)
    nb = -(-n_assign // MOE_TM) + n_exp
    tok_of_slot = jnp.zeros((nb * MOE_TM,), jnp.int32).at[slot_sorted].set(order // TOP_K)
    slot_of_assign = jnp.zeros((n_assign,), jnp.int32).at[order].set(slot_sorted).reshape(n_tok, TOP_K)
    block_start = jnp.arange(nb, dtype=jnp.int32) * MOE_TM
    valid = block_start < pad_end[-1]
    be = jnp.minimum(jnp.searchsorted(pad_end, block_start, side='right'), n_exp - 1).astype(jnp.int32)
    n_valid = pad_end[-1] // MOE_TM
    be = jnp.where(valid, be, be[jnp.maximum(n_valid - 1, 0)])
    first = jnp.logical_and(valid, block_start == pad_start[be])
    flags = valid.astype(jnp.int32) + 2 * first.astype(jnp.int32)
    return tok_of_slot, slot_of_assign, be, flags


def _moe(h2, topi, gates, w_gu, b_gu, w_down, b_down):
    bsz, s, d = h2.shape
    n_exp = w_gu.shape[0]
    topi_t = topi.transpose(0, 2, 1).reshape(bsz * s, TOP_K)
    gates_t = gates.transpose(0, 2, 1).reshape(bsz * s, TOP_K)
    tok_of_slot, slot_of_assign, be, flags = _route(topi_t, n_exp)
    xg = jnp.take(h2.reshape(bsz * s, d), tok_of_slot, axis=0)
    y = _moe_experts(xg, be, flags, w_gu, b_gu, w_down, b_down)
    picked = jnp.take(y, slot_of_assign.reshape(-1), axis=0).reshape(bsz * s, TOP_K, d).astype(F32)
    return jnp.sum(picked * gates_t[:, :, None], axis=1).reshape(bsz, s, d)


def _final_kernel(x_ref, f_ref, gm_ref, g_ref, o_ref):
    x = x_ref[...] + gm_ref[...] * f_ref[...]
    ms = jnp.mean(x * x, axis=-1, keepdims=True)
    o_ref[...] = x * lax.rsqrt(ms + EPS) * g_ref[...]


def _final_norm(x_mid, f, gate_mlp, final_g):
    bsz, s, d = x_mid.shape
    tok = pl.BlockSpec((None, TILE, d), lambda b, t: (b, t, 0))
    return pl.pallas_call(
        _final_kernel,
        grid=(bsz, s // TILE),
        in_specs=[tok, tok, pl.BlockSpec((None, 1, d), lambda b, t: (b, 0, 0)), _full_spec((1, d))],
        out_specs=tok,
        out_shape=jax.ShapeDtypeStruct((bsz, s, d), F32),
        compiler_params=_cparams(2, 32),
        name="final_norm",
    )(x_mid, f, gate_mlp, final_g.astype(F32).reshape(1, d))


def kernel(x, c, ctx, c_ctx, w_ada, b_ada, norm1_g, norm2_g, ssd_w_in, ssd_w_conv, ssd_b_conv, ssd_dt_bias, ssd_a_log, ssd_d, ssd_norm_g, ssd_w_out, cv_w_pw1, cv_b_pw1, cv_w_dw, cv_b_dw, cv_ln_g, cv_ln_b, cv_w_pw2, cv_b_pw2, moe_w_router, moe_b_router, moe_w_gu, moe_b_gu, moe_w_down, moe_b_down, final_g):
    bsz, seq, d = x.shape
    n_ctx = ctx.shape[1]
    depth = w_ada.shape[0]
    assert n_ctx == TILE and seq % TILE == 0 and TILE % GRID_W == 0 and TILE % SSD_CHUNK == 0
    assert depth % N_MIXERS == 0

    mod_rows = -(-(bsz + 1) // 8) * 8
    c_all = jnp.zeros((mod_rows, d), F32).at[:bsz].set(c).at[bsz].set(c_ctx)
    mods = _ada_mods(c_all, w_ada, b_ada)
    xs = jnp.concatenate([ctx, x], axis=1)
    is_ctx = (jnp.arange(n_ctx + seq) < n_ctx)[None, :, None]

    for i in range(depth):
        j = i // N_MIXERS
        use_ssd = (i % N_MIXERS) == 0
        need_ctx = i < depth - 1
        mod_i = mods[i]
        if use_ssd:
            z_g, xbc, dt, acum, acum_t = _ssd_inproj(xs, mod_i, norm1_g[i], ssd_w_in[j], ssd_dt_bias[j],
                                                     ssd_a_log[j])
            xs_g, bm_g, cm_g = _ssd_conv(xbc, ssd_w_conv[j], ssd_b_conv[j])
            s = xs.shape[1]
            by_dir = lambda v: v.reshape(bsz, s, 2, N_SSD_HEADS).transpose(2, 0, 1, 3)
            acum_t_d = acum_t.reshape(bsz, 2, N_SSD_HEADS, s).transpose(1, 0, 2, 3)
            y = _ssd_scan(xs_g, bm_g, cm_g, by_dir(dt), by_dir(acum), acum_t_d, n_ctx // SSD_CHUNK)
            x_mid, h2, topi, gates = _ssd_out(xs, mod_i, y, xs_g, z_g, ssd_d[j], ssd_norm_g[j], ssd_w_out[j],
                                              norm2_g[i], moe_w_router[i], moe_b_router[i])
        else:
            x_mid, h2, topi, gates = _conformer(xs, mod_i, norm1_g[i], cv_w_pw1[j], cv_b_pw1[j], cv_w_dw[j],
                                                cv_b_dw[j], cv_ln_g[j], cv_ln_b[j], cv_w_pw2[j], cv_b_pw2[j],
                                                norm2_g[i], moe_w_router[i], moe_b_router[i], need_ctx)
        f = _moe(h2, topi, gates, moe_w_gu[i], moe_b_gu[i], moe_w_down[i], moe_b_down[i])
        gate_lat = mod_i[:bsz, 5][:, None, :]
        if need_ctx:
            gate_mlp = jnp.where(is_ctx, mod_i[bsz, 5][None, None, :], gate_lat)
            xs = x_mid + gate_mlp * f
        else:
            return _final_norm(x_mid, f, gate_lat, final_g)
```

```python
import functools

import jax
import jax.numpy as jnp
from jax import lax
from jax.experimental import pallas as pl
from jax.experimental.pallas import tpu as pltpu

F32 = jnp.float32
BF16 = jnp.bfloat16
MXU_DTYPE = BF16
ACT_DTYPE = BF16

EPS = 1e-6
GRID_W = 64
N_MIXERS = 2
HEAD_DIM = 64
N_SSD_HEADS = 32
N_SSD_GROUPS = 8
HEADS_PER_GROUP = N_SSD_HEADS // N_SSD_GROUPS
D_STATE = 128
SSD_CHUNK = 128
GROUP_W = HEADS_PER_GROUP * HEAD_DIM
D_INNER = N_SSD_HEADS * HEAD_DIM
D_BC = N_SSD_GROUPS * D_STATE
D_XBC = D_INNER + 2 * D_BC
TOP_K = 4
SWIGLU_ALPHA = 1.702
SWIGLU_LIMIT = 7.0

TILE = 256
CONV_PAD = 16
SSD_HALO = 8
N_CHUNK = 512
MOE_TM = 512
CNT_LANES = 128
VMEM_MB = 1024 * 1024


def _cparams(n_axes, vmem_mb):
    return pltpu.CompilerParams(dimension_semantics=("arbitrary",) * n_axes,
                                vmem_limit_bytes=vmem_mb * VMEM_MB)


def _full_spec(shape):
    zeros = (0,) * len(shape)
    return pl.BlockSpec(shape, lambda *_: zeros)


def _dot(a, b):
    return jnp.dot(a.astype(MXU_DTYPE), b.astype(MXU_DTYPE), preferred_element_type=F32)


def _dot_nt(a, b):
    return lax.dot_general(a.astype(MXU_DTYPE), b.astype(MXU_DTYPE), (((1,), (1,)), ((), ())),
                           preferred_element_type=F32)


def _dot_tn(a, b):
    return lax.dot_general(a.astype(MXU_DTYPE), b.astype(MXU_DTYPE), (((0,), (0,)), ((), ())),
                           preferred_element_type=F32)


def _split3(v):
    hi = v.astype(BF16)
    r1 = v - hi.astype(F32)
    mid = r1.astype(BF16)
    lo = (r1 - mid.astype(F32)).astype(BF16)
    return hi, mid, lo


def _dot01_left(m01, v):
    hi, mid, lo = _split3(v)
    m = m01.astype(BF16)
    d = lambda p: jnp.dot(m, p, preferred_element_type=F32)
    return d(hi) + d(mid) + d(lo)


def _dot01_right(v, m01):
    hi, mid, lo = _split3(v)
    m = m01.astype(BF16)
    d = lambda p: jnp.dot(p, m, preferred_element_type=F32)
    return d(hi) + d(mid) + d(lo)


def _dot_nt_precise(a, b):
    ah = a.astype(BF16)
    al = (a - ah.astype(F32)).astype(BF16)
    bh = b.astype(BF16)
    bl = (b - bh.astype(F32)).astype(BF16)
    d = lambda p, q: lax.dot_general(p, q, (((1,), (1,)), ((), ())), preferred_element_type=F32)
    return d(ah, bh) + d(ah, bl) + d(al, bh)


def _sigmoid(v):
    return 1.0 / (1.0 + jnp.exp(-v))


def _silu(v):
    return v * _sigmoid(v)


def _softplus(v):
    return jnp.maximum(v, 0.0) + jnp.log(1.0 + jnp.exp(-jnp.abs(v)))


def _rms_mod(x, g, shift, scale):
    ms = jnp.mean(x * x, axis=-1, keepdims=True)
    y = x * lax.rsqrt(ms + EPS) * g
    return y * (1.0 + scale) + shift


def _ada_kernel(c_ref, w_ref, b_ref, o_ref):
    o_ref[...] = _dot(_silu(c_ref[...]), w_ref[...]) + b_ref[...]


def _ada_mods(c_all, w_ada, b_ada):
    depth, d, d6 = w_ada.shape
    n_mod = d6 // d
    rows = c_all.shape[0]
    out = pl.pallas_call(
        _ada_kernel,
        grid=(depth, n_mod),
        in_specs=[
            pl.BlockSpec((rows, d), lambda i, n: (0, 0)),
            pl.BlockSpec((None, d, d), lambda i, n: (i, 0, n)),
            pl.BlockSpec((None, None, 1, d), lambda i, n: (i, n, 0, 0)),
        ],
        out_specs=pl.BlockSpec((None, None, rows, d), lambda i, n: (i, n, 0, 0)),
        out_shape=jax.ShapeDtypeStruct((depth, n_mod, rows, d), F32),
        compiler_params=_cparams(2, 32),
        name="ada_mods",
    )(c_all, w_ada, b_ada.reshape(depth, n_mod, 1, d))
    return out.transpose(0, 2, 1, 3)


def _moe_pre(x_new, mod, g2, wr_t, b_r, h2_ref, topi_ref, gate_ref, rank_ref, cnt_ref, cnt_scr):
    @pl.when(jnp.logical_and(pl.program_id(0) == 0, pl.program_id(1) == 0))
    def _():
        cnt_scr[...] = jnp.zeros_like(cnt_scr)

    h2 = _rms_mod(x_new, g2, mod[3:4], mod[4:5])
    h2_ref[...] = h2.astype(h2_ref.dtype)
    logits = _dot_nt_precise(wr_t, h2) + b_r
    n_exp = logits.shape[0]
    eidx = lax.broadcasted_iota(jnp.int32, logits.shape, 0)
    vals, idxs = [], []
    cur = logits
    for _ in range(TOP_K):
        m = jnp.max(cur, axis=0, keepdims=True)
        idx = jnp.min(jnp.where(cur == m, eidx, n_exp), axis=0, keepdims=True)
        vals.append(m)
        idxs.append(idx)
        cur = jnp.where(eidx == idx, -jnp.inf, cur)
    es = [jnp.exp(v - vals[0]) for v in vals]
    tot = es[0] + es[1] + es[2] + es[3]
    rows = logits.shape[1]
    t_src = lax.broadcasted_iota(jnp.int32, (rows, rows), 0)
    t_dst = lax.broadcasted_iota(jnp.int32, (rows, rows), 1)
    earlier = jnp.where(t_src < t_dst, 1.0, 0.0).astype(BF16)
    base = cnt_scr[:, 0:1]
    for k in range(TOP_K):
        topi_ref[k:k + 1, :] = idxs[k]
        gate_ref[k:k + 1, :] = es[k] / tot
        onehot = jnp.where(eidx == idxs[k], 1.0, 0.0)
        before = jnp.dot(onehot.astype(BF16), earlier, preferred_element_type=F32)
        rank = jnp.sum(onehot * (base + before), axis=0, keepdims=True)
        rank_ref[k:k + 1, :] = rank.astype(jnp.int32)
        base = base + jnp.sum(onehot, axis=1, keepdims=True)
    totals = jnp.broadcast_to(base, cnt_scr.shape)
    cnt_scr[...] = totals
    cnt_ref[...] = totals


def _moe_pre_specs(d, n_exp):
    in_specs = [_full_spec((1, d)), _full_spec((n_exp, d)), _full_spec((n_exp, 1))]
    out_specs = [
        pl.BlockSpec((None, TILE, d), lambda b, t: (b, t, 0)),
        pl.BlockSpec((None, TILE, d), lambda b, t: (b, t, 0)),
        pl.BlockSpec((None, TOP_K, TILE), lambda b, t: (b, 0, t)),
        pl.BlockSpec((None, TOP_K, TILE), lambda b, t: (b, 0, t)),
        pl.BlockSpec((None, TOP_K, TILE), lambda b, t: (b, 0, t)),
        _full_spec((n_exp, CNT_LANES)),
    ]
    return in_specs, out_specs


def _moe_pre_shapes(bsz, s_out, d, n_exp):
    return [
        jax.ShapeDtypeStruct((bsz, s_out, d), F32),
        jax.ShapeDtypeStruct((bsz, s_out, d), ACT_DTYPE),
        jax.ShapeDtypeStruct((bsz, TOP_K, s_out), jnp.int32),
        jax.ShapeDtypeStruct((bsz, TOP_K, s_out), F32),
        jax.ShapeDtypeStruct((bsz, TOP_K, s_out), jnp.int32),
        jax.ShapeDtypeStruct((n_exp, CNT_LANES), F32),
    ]


def _moe_pre_scratch(n_exp):
    return pltpu.VMEM((n_exp, CNT_LANES), F32)


def _inproj_kernel(x_ref, mod_ref, g_ref, wz_ref, wx_ref, wdt_ref, wdt_t_ref, dtb_ref, dtb_t_ref,
                   a_ref, a_t_ref, z_ref, xbc_ref, dt_ref, acum_ref, acum_t_ref):
    mod = mod_ref[...]
    h = _rms_mod(x_ref[...], g_ref[...], mod[0:1], mod[1:2]).astype(MXU_DTYPE)
    for g in range(N_SSD_GROUPS):
        z_ref[g] = _dot(h, wz_ref[:, g * GROUP_W:(g + 1) * GROUP_W]).astype(z_ref.dtype)
    for n in range(0, D_XBC, N_CHUNK):
        xbc_ref[:, n:n + N_CHUNK] = _dot(h, wx_ref[:, n:n + N_CHUNK]).astype(xbc_ref.dtype)
    dt = _softplus(_dot(h, wdt_ref[...]) + dtb_ref[...])
    dt_t = _softplus(_dot_nt(wdt_t_ref[...], h) + dtb_t_ref[...])
    dt_ref[...] = dt
    dta = dt * a_ref[...]
    dta_t = dt_t * a_t_ref[...]
    ii = lax.broadcasted_iota(jnp.int32, (TILE, TILE), 0)
    jj = lax.broadcasted_iota(jnp.int32, (TILE, TILE), 1)
    same = (ii // SSD_CHUNK) == (jj // SSD_CHUNK)
    lower = jnp.where(same, jnp.where(jj <= ii, 1.0, 0.0), 0.0)
    upper = jnp.where(same, jnp.where(jj >= ii, 1.0, 0.0), 0.0)
    col = lax.broadcasted_iota(jnp.int32, dta.shape, 1)
    acum_ref[...] = jnp.where(col < N_SSD_HEADS, _dot01_left(lower, dta), _dot01_left(upper, dta))
    row = lax.broadcasted_iota(jnp.int32, dta_t.shape, 0)
    acum_t_ref[...] = jnp.where(row < N_SSD_HEADS, _dot01_right(dta_t, upper), _dot01_right(dta_t, lower))


def _ssd_inproj(xs, mod_i, g1, w_in, dt_bias, a_log):
    bsz, s, d = xs.shape
    nt = s // TILE
    ctx_row = bsz
    wz = w_in[:, :D_INNER].astype(MXU_DTYPE)
    wx = w_in[:, D_INNER:D_INNER + D_XBC].astype(MXU_DTYPE)
    wdt = w_in[:, D_INNER + D_XBC:].astype(MXU_DTYPE)
    n_dt = 2 * N_SSD_HEADS
    a = -jnp.exp(a_log.astype(F32)).reshape(1, n_dt)
    dtb = dt_bias.astype(F32).reshape(1, n_dt)
    tok = lambda w: pl.BlockSpec((None, TILE, w), lambda b, t: (b, t, 0))
    return pl.pallas_call(
        _inproj_kernel,
        grid=(bsz, nt),
        in_specs=[
            tok(d),
            pl.BlockSpec((None, 6, d), lambda b, t: (jnp.where(t == 0, ctx_row, b), 0, 0)),
            _full_spec((1, d)),
            _full_spec((d, D_INNER)), _full_spec((d, D_XBC)), _full_spec((d, n_dt)), _full_spec((n_dt, d)),
            _full_spec((1, n_dt)), _full_spec((n_dt, 1)), _full_spec((1, n_dt)), _full_spec((n_dt, 1)),
        ],
        out_specs=[
            pl.BlockSpec((None, N_SSD_GROUPS, TILE, GROUP_W), lambda b, t: (b, 0, t, 0)),
            tok(D_XBC), tok(n_dt), tok(n_dt),
            pl.BlockSpec((None, n_dt, TILE), lambda b, t: (b, 0, t)),
        ],
        out_shape=[
            jax.ShapeDtypeStruct((bsz, N_SSD_GROUPS, s, GROUP_W), ACT_DTYPE),
            jax.ShapeDtypeStruct((bsz, s, D_XBC), ACT_DTYPE),
            jax.ShapeDtypeStruct((bsz, s, n_dt), F32),
            jax.ShapeDtypeStruct((bsz, s, n_dt), F32),
            jax.ShapeDtypeStruct((bsz, n_dt, s), F32),
        ],
        compiler_params=_cparams(2, 56),
        name="ssd_inproj",
    )(xs, mod_i, g1.reshape(1, d), wz, wx, wdt, wdt.T, dtb, dtb.T, a, a.T)


def _ssd_conv_kernel(nt, main_ref, prev_ref, next_ref, w_ref, b_ref, xs_ref, bm_ref, cm_ref, buf):
    t = pl.program_id(1)
    width = w_ref.shape[0]
    half = (width - 1) // 2
    prev_ok = jnp.where(t >= 2, 1.0, 0.0)
    next_ok = jnp.where(jnp.logical_and(t >= 1, t < nt - 1), 1.0, 0.0)
    buf[0:SSD_HALO, :] = prev_ref[...].astype(F32) * prev_ok
    buf[SSD_HALO:SSD_HALO + TILE, :] = main_ref[...].astype(F32)
    buf[SSD_HALO + TILE:, :] = next_ref[...].astype(F32) * next_ok
    rows = 64
    for r0 in range(0, TILE, rows):
        for n in range(0, D_XBC, N_CHUNK):
            acc = jnp.zeros((rows, N_CHUNK), F32) + b_ref[:, n:n + N_CHUNK]
            for k in range(width):
                off = SSD_HALO + r0 + k - half
                acc = acc + w_ref[k:k + 1, n:n + N_CHUNK] * buf[off:off + rows, n:n + N_CHUNK]
            out = _silu(acc)
            for c0 in range(n, n + N_CHUNK, D_STATE):
                piece = out[:, c0 - n:c0 - n + D_STATE]
                if c0 < D_INNER:
                    g, o = divmod(c0, GROUP_W)
                    xs_ref[g, r0:r0 + rows, o:o + D_STATE] = piece.astype(xs_ref.dtype)
                elif c0 < D_INNER + D_BC:
                    bm_ref[(c0 - D_INNER) // D_STATE, r0:r0 + rows, :] = piece.astype(bm_ref.dtype)
                else:
                    cm_ref[(c0 - D_INNER - D_BC) // D_STATE, r0:r0 + rows, :] = piece.astype(cm_ref.dtype)


def _ssd_conv(xbc, w_conv, b_conv):
    bsz, s, c = xbc.shape
    nt = s // TILE
    per_tile = TILE // SSD_HALO
    last_halo = s // SSD_HALO - 1
    grp = lambda w: pl.BlockSpec((None, N_SSD_GROUPS, TILE, w), lambda b, t: (b, 0, t, 0))
    return pl.pallas_call(
        functools.partial(_ssd_conv_kernel, nt),
        grid=(bsz, nt),
        in_specs=[
            pl.BlockSpec((None, TILE, c), lambda b, t: (b, t, 0)),
            pl.BlockSpec((None, SSD_HALO, c), lambda b, t: (b, jnp.maximum(t * per_tile - 1, 0), 0)),
            pl.BlockSpec((None, SSD_HALO, c), lambda b, t: (b, jnp.minimum((t + 1) * per_tile, last_halo), 0)),
            _full_spec(w_conv.shape), _full_spec((1, c)),
        ],
        out_specs=[grp(GROUP_W), grp(D_STATE), grp(D_STATE)],
        out_shape=[
            jax.ShapeDtypeStruct((bsz, N_SSD_GROUPS, s, GROUP_W), ACT_DTYPE),
            jax.ShapeDtypeStruct((bsz, N_SSD_GROUPS, s, D_STATE), ACT_DTYPE),
            jax.ShapeDtypeStruct((bsz, N_SSD_GROUPS, s, D_STATE), ACT_DTYPE),
        ],
        scratch_shapes=[pltpu.VMEM((TILE + 2 * SSD_HALO, c), F32)],
        compiler_params=_cparams(2, 48),
        name="ssd_conv",
    )(xbc, xbc, xbc, w_conv.astype(F32), b_conv.astype(F32).reshape(1, c))


def _ssd_scan_kernel(xs_ref, bm_ref, cm_ref, dt_ref, acol_ref, arow_ref, y_ref, st_ref):
    d = pl.program_id(0)
    c = pl.program_id(2)

    @pl.when(c == 0)
    def _():
        st_ref[...] = jnp.zeros_like(st_ref)

    q = SSD_CHUNK
    ii = lax.broadcasted_iota(jnp.int32, (q, q), 0)
    jj = lax.broadcasted_iota(jnp.int32, (q, q), 1)
    mask = (ii - jj) * (1 - 2 * d) >= 0
    head_of_lane = lax.broadcasted_iota(jnp.int32, (1, GROUP_W), 1) // HEAD_DIM

    def expand(v):
        out = v[:, HEADS_PER_GROUP - 1:HEADS_PER_GROUP]
        for r in range(HEADS_PER_GROUP - 2, -1, -1):
            out = jnp.where(head_of_lane == r, v[:, r:r + 1], out)
        return out

    for g in range(N_SSD_GROUPS):
        hs = slice(g * HEADS_PER_GROUP, (g + 1) * HEADS_PER_GROUP)
        x = xs_ref[g].astype(F32)
        bm = bm_ref[g]
        cm = cm_ref[g]
        dtc = dt_ref[:, hs]
        acol = acol_ref[:, hs]
        arow = arow_ref[hs, :]
        a_tot = jnp.where(d == 0, acol[q - 1:q, :], acol[0:1, :])
        scores = _dot_nt(cm, bm)
        st = st_ref[g]
        xdt = x * expand(dtc)
        xdt_m = xdt.astype(MXU_DTYPE)
        y = _dot(cm, st) * expand(jnp.exp(acol))
        for r in range(HEADS_PER_GROUP):
            seg = acol[:, r:r + 1] - arow[r:r + 1, :]
            decay = jnp.where(mask, jnp.exp(seg), 0.0)
            yr = _dot(scores * decay, xdt_m)
            y = y + jnp.where(head_of_lane == r, yr, 0.0)
        y_ref[g] = y.astype(y_ref.dtype)
        xdte = xdt * expand(jnp.exp(a_tot - acol))
        st_ref[g] = st * expand(jnp.exp(a_tot)) + _dot_tn(bm, xdte)


def _ssd_scan(xs_g, bm_g, cm_g, dt_d, acum_d, acum_t_d, n_ctx_chunks):
    bsz, n_grp, s, _ = xs_g.shape
    nc = s // SSD_CHUNK
    last = nc - 1 + n_ctx_chunks

    def chunk(d, c):
        back = jnp.where(c < n_ctx_chunks, n_ctx_chunks - 1 - c, last - c)
        return jnp.where(d == 0, c, back)

    big = lambda w: pl.BlockSpec((None, n_grp, SSD_CHUNK, w), lambda d, b, c: (b, 0, chunk(d, c), 0))
    col = pl.BlockSpec((None, None, SSD_CHUNK, N_SSD_HEADS), lambda d, b, c: (d, b, chunk(d, c), 0))
    return pl.pallas_call(
        _ssd_scan_kernel,
        grid=(2, bsz, nc),
        in_specs=[
            big(GROUP_W), big(D_STATE), big(D_STATE), col, col,
            pl.BlockSpec((None, None, N_SSD_HEADS, SSD_CHUNK), lambda d, b, c: (d, b, 0, chunk(d, c))),
        ],
        out_specs=pl.BlockSpec((None, None, n_grp, SSD_CHUNK, GROUP_W),
                               lambda d, b, c: (d, b, 0, chunk(d, c), 0)),
        out_shape=jax.ShapeDtypeStruct((2, bsz, n_grp, s, GROUP_W), ACT_DTYPE),
        scratch_shapes=[pltpu.VMEM((n_grp, D_STATE, GROUP_W), F32)],
        compiler_params=_cparams(3, 32),
        name="ssd_scan",
    )(xs_g, bm_g, cm_g, dt_d, acum_d, acum_t_d)


def _ssd_out_kernel(x_ref, mod_ref, yf_ref, yb_ref, xs_ref, z_ref, dsk_ref, ng_ref, wo_ref,
                    g2_ref, wr_ref, br_ref, xo_ref, h2_ref, topi_ref, gate_ref, rank_ref, cnt_ref,
                    yn_scr, cnt_scr):
    mod = mod_ref[...]
    for g in range(N_SSD_GROUPS):
        y = yf_ref[g].astype(F32) + yb_ref[g].astype(F32) + dsk_ref[g] * xs_ref[g].astype(F32)
        y = y * _silu(z_ref[g].astype(F32))
        ms = jnp.mean(y * y, axis=-1, keepdims=True)
        yn_scr[:, g * GROUP_W:(g + 1) * GROUP_W] = (y * lax.rsqrt(ms + EPS) * ng_ref[g]).astype(yn_scr.dtype)
    x_new = x_ref[...] + mod[2:3] * _dot(yn_scr[...], wo_ref[...])
    xo_ref[...] = x_new
    _moe_pre(x_new, mod, g2_ref[...], wr_ref[...], br_ref[...], h2_ref, topi_ref, gate_ref, rank_ref, cnt_ref,
             cnt_scr)


def _ssd_out(xs, mod_i, y, xs_g, z_g, d_skip, norm_g, w_out, g2, w_router, b_router):
    bsz, s, d = xs.shape
    nt = s // TILE
    ctx_row = bsz
    n_exp = w_router.shape[1]
    grp = lambda: pl.BlockSpec((None, N_SSD_GROUPS, TILE, GROUP_W), lambda b, t: (b, 0, t, 0))
    ydir = lambda dd: pl.BlockSpec((None, None, N_SSD_GROUPS, TILE, GROUP_W), lambda b, t: (dd, b, 0, t, 0))
    pre_in, pre_out = _moe_pre_specs(d, n_exp)
    dsk = jnp.repeat(d_skip.astype(F32), HEAD_DIM).reshape(N_SSD_GROUPS, 1, GROUP_W)
    return pl.pallas_call(
        _ssd_out_kernel,
        grid=(bsz, nt),
        in_specs=[
            pl.BlockSpec((None, TILE, d), lambda b, t: (b, t, 0)),
            pl.BlockSpec((None, 6, d), lambda b, t: (jnp.where(t == 0, ctx_row, b), 0, 0)),
            ydir(0), ydir(1), grp(), grp(),
            _full_spec((N_SSD_GROUPS, 1, GROUP_W)), _full_spec((N_SSD_GROUPS, 1, GROUP_W)),
            _full_spec((D_INNER, d)),
        ] + pre_in,
        out_specs=pre_out,
        out_shape=_moe_pre_shapes(bsz, s, d, n_exp),
        scratch_shapes=[pltpu.VMEM((TILE, D_INNER), MXU_DTYPE), _moe_pre_scratch(n_exp)],
        compiler_params=_cparams(2, 48),
        name="ssd_out",
    )(xs, mod_i, y, y, xs_g, z_g, dsk, norm_g.astype(F32).reshape(N_SSD_GROUPS, 1, GROUP_W),
      w_out.astype(MXU_DTYPE), g2.reshape(1, d), w_router.T.astype(F32), b_router.astype(F32).reshape(n_exp, 1))


def _cv_kernel(t0, x_ref, mod_ref, g1_ref, w1_ref, b1_ref, wdw_ref, bdw_ref, lng_ref, lnb_ref, w2_ref, b2_ref,
               g2_ref, wr_ref, br_ref, xo_ref, h2_ref, topi_ref, gate_ref, rank_ref, cnt_ref,
               pad_scr, cv_scr, cnt_scr):
    tile = pl.program_id(1) + t0
    d = x_ref.shape[-1]
    x = x_ref[...]
    mod = mod_ref[...]
    h = _rms_mod(x, g1_ref[...], mod[0:1], mod[1:2]).astype(MXU_DTYPE)
    width = wdw_ref.shape[0]
    half = (width - 1) // 2
    seg = GRID_W
    n_seg = TILE // seg
    stride = seg + 2 * CONV_PAD
    joined = jnp.where(tile == 0, 1.0, 0.0)
    zeros = jnp.zeros((CONV_PAD, d), F32)
    pad_scr[0:CONV_PAD, :] = zeros
    pad_scr[n_seg * stride - CONV_PAD:n_seg * stride, :] = zeros
    for n in range(0, d, N_CHUNK):
        a = _dot(h, w1_ref[:, n:n + N_CHUNK]) + b1_ref[:, n:n + N_CHUNK]
        gate = _dot(h, w1_ref[:, d + n:d + n + N_CHUNK]) + b1_ref[:, d + n:d + n + N_CHUNK]
        u = a * _sigmoid(gate)
        for s_i in range(n_seg):
            base = s_i * stride
            pad_scr[base + CONV_PAD:base + CONV_PAD + seg, n:n + N_CHUNK] = u[s_i * seg:(s_i + 1) * seg]
            if s_i > 0:
                pad_scr[base:base + CONV_PAD, n:n + N_CHUNK] = u[s_i * seg - CONV_PAD:s_i * seg] * joined
            if s_i < n_seg - 1:
                pad_scr[base + CONV_PAD + seg:base + stride, n:n + N_CHUNK] = (
                    u[(s_i + 1) * seg:(s_i + 1) * seg + CONV_PAD] * joined)
    lanes = 256
    for s_i in range(n_seg):
        for n in range(0, d, lanes):
            acc = jnp.zeros((seg, lanes), F32) + bdw_ref[:, n:n + lanes]
            for k in range(width):
                off = s_i * stride + CONV_PAD + k - half
                acc = acc + wdw_ref[k:k + 1, n:n + lanes] * pad_scr[off:off + seg, n:n + lanes]
            cv_scr[s_i * seg:(s_i + 1) * seg, n:n + lanes] = acc
    cv = cv_scr[...]
    mu = jnp.mean(cv, axis=-1, keepdims=True)
    xc = cv - mu
    var = jnp.mean(xc * xc, axis=-1, keepdims=True)
    ln = xc * lax.rsqrt(var + EPS) * lng_ref[...] + lnb_ref[...]
    out = _dot(_silu(ln), w2_ref[...]) + b2_ref[...]
    x_new = x + mod[2:3] * out
    xo_ref[...] = x_new
    _moe_pre(x_new, mod, g2_ref[...], wr_ref[...], br_ref[...], h2_ref, topi_ref, gate_ref, rank_ref, cnt_ref,
             cnt_scr)


def _conformer(xs, mod_i, g1, w_pw1, b_pw1, w_dw, b_dw, ln_g, ln_b, w_pw2, b_pw2, g2, w_router, b_router,
               with_ctx):
    bsz, s, d = xs.shape
    t0 = 0 if with_ctx else 1
    nt = s // TILE - t0
    ctx_row = bsz
    n_exp = w_router.shape[1]
    pre_in, pre_out = _moe_pre_specs(d, n_exp)
    row = lambda v: v.astype(F32).reshape(1, -1)
    return pl.pallas_call(
        functools.partial(_cv_kernel, t0),
        grid=(bsz, nt),
        in_specs=[
            pl.BlockSpec((None, TILE, d), lambda b, t: (b, t + t0, 0)),
            pl.BlockSpec((None, 6, d), lambda b, t: (jnp.where(t + t0 == 0, ctx_row, b), 0, 0)),
            _full_spec((1, d)),
            _full_spec((d, 2 * d)), _full_spec((1, 2 * d)),
            _full_spec(w_dw.shape), _full_spec((1, d)), _full_spec((1, d)), _full_spec((1, d)),
            _full_spec((d, d)), _full_spec((1, d)),
        ] + pre_in,
        out_specs=pre_out,
        out_shape=_moe_pre_shapes(bsz, nt * TILE, d, n_exp),
        scratch_shapes=[
            pltpu.VMEM((TILE + 2 * CONV_PAD * (TILE // GRID_W), d), F32),
            pltpu.VMEM((TILE, d), F32),
            _moe_pre_scratch(n_exp),
        ],
        compiler_params=_cparams(2, 48),
        name="conformer",
    )(xs, mod_i, row(g1), w_pw1.astype(MXU_DTYPE), row(b_pw1), w_dw.astype(F32), row(b_dw), row(ln_g), row(ln_b),
      w_pw2.astype(MXU_DTYPE), row(b_pw2), row(g2), w_router.T.astype(F32),
      b_router.astype(F32).reshape(n_exp, 1))


def _moe_kernel(be_ref, flag_ref, x_ref, wgu_ref, bgu_ref, wdn_ref, bdn_ref, y_ref, wgu_scr, wdn_scr, act_scr):
    i = pl.program_id(0)
    flags = flag_ref[i]
    d_ff = wdn_ref.shape[0]

    @pl.when((flags & 2) != 0)
    def _():
        for n in range(0, 2 * d_ff, N_CHUNK):
            wgu_scr[:, n:n + N_CHUNK] = wgu_ref[:, n:n + N_CHUNK].astype(wgu_scr.dtype)
        for n in range(0, wdn_ref.shape[1], N_CHUNK):
            wdn_scr[:, n:n + N_CHUNK] = wdn_ref[:, n:n + N_CHUNK].astype(wdn_scr.dtype)

    @pl.when((flags & 1) != 0)
    def _():
        x = x_ref[...]
        for n in range(0, d_ff, N_CHUNK):
            gate = _dot(x, wgu_scr[:, n:n + N_CHUNK]) + bgu_ref[:, n:n + N_CHUNK]
            up = _dot(x, wgu_scr[:, d_ff + n:d_ff + n + N_CHUNK]) + bgu_ref[:, d_ff + n:d_ff + n + N_CHUNK]
            gate = jnp.minimum(gate, SWIGLU_LIMIT)
            up = jnp.clip(up, -SWIGLU_LIMIT, SWIGLU_LIMIT)
            glu = gate * _sigmoid(gate * SWIGLU_ALPHA)
            act_scr[:, n:n + N_CHUNK] = ((up + 1.0) * glu).astype(act_scr.dtype)
        act = act_scr[...]
        for n in range(0, y_ref.shape[1], N_CHUNK):
            y_ref[:, n:n + N_CHUNK] = (_dot(act, wdn_scr[:, n:n + N_CHUNK])
                                       + bdn_ref[:, n:n + N_CHUNK]).astype(y_ref.dtype)

    @pl.when((flags & 1) == 0)
    def _():
        y_ref[...] = jnp.zeros_like(y_ref)


def _moe_experts(xg, block_expert, block_flags, layer, w_gu, b_gu, w_down, b_down):
    n_rows, d = xg.shape
    depth, n_exp, _, d_gu = w_gu.shape
    d_ff = w_down.shape[2]
    nb = n_rows // MOE_TM
    grid_spec = pltpu.PrefetchScalarGridSpec(
        num_scalar_prefetch=2,
        grid=(nb,),
        in_specs=[
            pl.BlockSpec((MOE_TM, d), lambda i, be, fl: (i, 0)),
            pl.BlockSpec((None, None, d, d_gu), lambda i, be, fl: (layer, be[i], 0, 0)),
            pl.BlockSpec((None, None, 1, d_gu), lambda i, be, fl: (layer, be[i], 0, 0)),
            pl.BlockSpec((None, None, d_ff, d), lambda i, be, fl: (layer, be[i], 0, 0)),
            pl.BlockSpec((None, None, 1, d), lambda i, be, fl: (layer, be[i], 0, 0)),
        ],
        out_specs=pl.BlockSpec((MOE_TM, d), lambda i, be, fl: (i, 0)),
        scratch_shapes=[
            pltpu.VMEM((d, d_gu), MXU_DTYPE),
            pltpu.VMEM((d_ff, d), MXU_DTYPE),
            pltpu.VMEM((MOE_TM, d_ff), MXU_DTYPE),
        ],
    )
    return pl.pallas_call(
        _moe_kernel,
        grid_spec=grid_spec,
        out_shape=jax.ShapeDtypeStruct((n_rows, d), ACT_DTYPE),
        compiler_params=_cparams(1, 56),
        name="moe_experts",
    )(block_expert, block_flags, xg, w_gu, b_gu.reshape(depth, n_exp, 1, d_gu), w_down,
      b_down.reshape(depth, n_exp, 1, d))


def _route(topi, rank, counts):
    n_exp = counts.shape[0]
    n_assign = topi.size
    experts = jnp.arange(n_exp, dtype=jnp.int32)
    padded = (counts + MOE_TM - 1) // MOE_TM * MOE_TM
    pad_end = jnp.cumsum(padded)
    pad_start = pad_end - padded
    unpad_start = jnp.cumsum(counts) - counts
    start_of = jnp.sum(jnp.where(topi[..., None] == experts, pad_start, 0), axis=-1)
    slot_of_assign = start_of + rank
    nb = -(-n_assign // MOE_TM) + n_exp
    block_start = jnp.arange(nb, dtype=jnp.int32) * MOE_TM
    valid = block_start < pad_end[-1]
    be = jnp.minimum(jnp.sum((block_start[:, None] >= pad_end[None, :]).astype(jnp.int32), axis=1), n_exp - 1)
    n_valid = pad_end[-1] // MOE_TM
    be = jnp.where(valid, be, be[jnp.maximum(n_valid - 1, 0)])
    first = jnp.logical_and(valid, block_start == pad_start[be])
    flags = valid.astype(jnp.int32) + 2 * first.astype(jnp.int32)
    by_slot = jnp.argsort(slot_of_assign.reshape(-1)).astype(jnp.int32)
    n_tok = topi.shape[1]
    e_slot = jnp.repeat(be, MOE_TM)
    r_slot = jnp.arange(nb * MOE_TM, dtype=jnp.int32) - pad_start[e_slot]
    occupied = jnp.logical_and(jnp.repeat(valid, MOE_TM), r_slot < counts[e_slot])
    src = jnp.clip(unpad_start[e_slot] + r_slot, 0, n_assign - 1)
    tok_of_slot = jnp.where(occupied, by_slot[src] % n_tok, 0)
    return tok_of_slot, slot_of_assign, be, flags


def _moe(h2, topi, gates, rank, counts, layer, w_gu, b_gu, w_down, b_down):
    bsz, s, d = h2.shape
    flat = lambda v: v.transpose(1, 0, 2).reshape(TOP_K, bsz * s)
    tok_of_slot, slot_of_assign, be, flags = _route(flat(topi), flat(rank), counts[:, 0].astype(jnp.int32))
    xg = jnp.take(h2.reshape(bsz * s, d), tok_of_slot, axis=0)
    y = _moe_experts(xg, be, flags, layer, w_gu, b_gu, w_down, b_down)
    gates_f = flat(gates)
    out = jnp.zeros((bsz * s, d), F32)
    for k in range(TOP_K):
        out = out + gates_f[k][:, None] * jnp.take(y, slot_of_assign[k], axis=0).astype(F32)
    return out.reshape(bsz, s, d)


def _final_kernel(x_ref, f_ref, gm_ref, g_ref, o_ref):
    x = x_ref[...] + gm_ref[...] * f_ref[...]
    ms = jnp.mean(x * x, axis=-1, keepdims=True)
    o_ref[...] = x * lax.rsqrt(ms + EPS) * g_ref[...]


def _final_norm(x_mid, f, gate_mlp, final_g):
    bsz, s, d = x_mid.shape
    tok = pl.BlockSpec((None, TILE, d), lambda b, t: (b, t, 0))
    return pl.pallas_call(
        _final_kernel,
        grid=(bsz, s // TILE),
        in_specs=[tok, tok, pl.BlockSpec((None, 1, d), lambda b, t: (b, 0, 0)), _full_spec((1, d))],
        out_specs=tok,
        out_shape=jax.ShapeDtypeStruct((bsz, s, d), F32),
        compiler_params=_cparams(2, 32),
        name="final_norm",
    )(x_mid, f, gate_mlp, final_g.astype(F32).reshape(1, d))


def kernel(x, c, ctx, c_ctx, w_ada, b_ada, norm1_g, norm2_g, ssd_w_in, ssd_w_conv, ssd_b_conv, ssd_dt_bias, ssd_a_log, ssd_d, ssd_norm_g, ssd_w_out, cv_w_pw1, cv_b_pw1, cv_w_dw, cv_b_dw, cv_ln_g, cv_ln_b, cv_w_pw2, cv_b_pw2, moe_w_router, moe_b_router, moe_w_gu, moe_b_gu, moe_w_down, moe_b_down, final_g):
    bsz, seq, d = x.shape
    n_ctx = ctx.shape[1]
    depth = w_ada.shape[0]
    assert n_ctx == TILE and seq % TILE == 0 and TILE % GRID_W == 0 and TILE % SSD_CHUNK == 0
    assert depth % N_MIXERS == 0

    mod_rows = -(-(bsz + 1) // 8) * 8
    c_all = jnp.zeros((mod_rows, d), F32).at[:bsz].set(c).at[bsz].set(c_ctx)
    mods = _ada_mods(c_all, w_ada, b_ada)
    xs = jnp.concatenate([ctx, x], axis=1)
    is_ctx = (jnp.arange(n_ctx + seq) < n_ctx)[None, :, None]

    for i in range(depth):
        j = i // N_MIXERS
        use_ssd = (i % N_MIXERS) == 0
        need_ctx = i < depth - 1
        mod_i = mods[i]
        if use_ssd:
            z_g, xbc, dt, acum, acum_t = _ssd_inproj(xs, mod_i, norm1_g[i], ssd_w_in[j], ssd_dt_bias[j],
                                                     ssd_a_log[j])
            xs_g, bm_g, cm_g = _ssd_conv(xbc, ssd_w_conv[j], ssd_b_conv[j])
            s = xs.shape[1]
            by_dir = lambda v: v.reshape(bsz, s, 2, N_SSD_HEADS).transpose(2, 0, 1, 3)
            acum_t_d = acum_t.reshape(bsz, 2, N_SSD_HEADS, s).transpose(1, 0, 2, 3)
            y = _ssd_scan(xs_g, bm_g, cm_g, by_dir(dt), by_dir(acum), acum_t_d, n_ctx // SSD_CHUNK)
            x_mid, h2, topi, gates, rank, counts = _ssd_out(
                xs, mod_i, y, xs_g, z_g, ssd_d[j], ssd_norm_g[j], ssd_w_out[j], norm2_g[i], moe_w_router[i],
                moe_b_router[i])
        else:
            x_mid, h2, topi, gates, rank, counts = _conformer(
                xs, mod_i, norm1_g[i], cv_w_pw1[j], cv_b_pw1[j], cv_w_dw[j], cv_b_dw[j], cv_ln_g[j], cv_ln_b[j],
                cv_w_pw2[j], cv_b_pw2[j], norm2_g[i], moe_w_router[i], moe_b_router[i], need_ctx)
        f = _moe(h2, topi, gates, rank, counts, i, moe_w_gu, moe_b_gu, moe_w_down, moe_b_down)
        gate_lat = mod_i[:bsz, 5][:, None, :]
        if need_ctx:
            gate_mlp = jnp.where(is_ctx, mod_i[bsz, 5][None, None, :], gate_lat)
            xs = x_mid + gate_mlp * f
        else:
            return _final_norm(x_mid, f, gate_lat, final_g)
```

```python
import functools

import jax
import jax.numpy as jnp
from jax import lax
from jax.experimental import pallas as pl
from jax.experimental.pallas import tpu as pltpu

F32 = jnp.float32
BF16 = jnp.bfloat16
MXU_DTYPE = BF16
ACT_DTYPE = BF16

EPS = 1e-6
GRID_W = 64
N_MIXERS = 2
HEAD_DIM = 64
N_SSD_HEADS = 32
N_SSD_GROUPS = 8
HEADS_PER_GROUP = N_SSD_HEADS // N_SSD_GROUPS
D_STATE = 128
SSD_CHUNK = 128
GROUP_W = HEADS_PER_GROUP * HEAD_DIM
D_INNER = N_SSD_HEADS * HEAD_DIM
D_BC = N_SSD_GROUPS * D_STATE
D_XBC = D_INNER + 2 * D_BC
TOP_K = 4
SWIGLU_ALPHA = 1.702
SWIGLU_LIMIT = 7.0

TILE = 256
CONV_PAD = 16
SSD_HALO = 8
N_CHUNK = 512
MOE_TM = 512
CNT_LANES = 128
SUBLANES = 8
VMEM_MB = 1024 * 1024


def _cparams(n_axes, vmem_mb):
    return pltpu.CompilerParams(dimension_semantics=("arbitrary",) * n_axes,
                                vmem_limit_bytes=vmem_mb * VMEM_MB)


def _full_spec(shape):
    zeros = (0,) * len(shape)
    return pl.BlockSpec(shape, lambda *_: zeros)


def _dot(a, b):
    return jnp.dot(a.astype(MXU_DTYPE), b.astype(MXU_DTYPE), preferred_element_type=F32)


def _dot_nt(a, b):
    return lax.dot_general(a.astype(MXU_DTYPE), b.astype(MXU_DTYPE), (((1,), (1,)), ((), ())),
                           preferred_element_type=F32)


def _dot_tn(a, b):
    return lax.dot_general(a.astype(MXU_DTYPE), b.astype(MXU_DTYPE), (((0,), (0,)), ((), ())),
                           preferred_element_type=F32)


def _split3(v):
    hi = v.astype(BF16)
    r1 = v - hi.astype(F32)
    mid = r1.astype(BF16)
    lo = (r1 - mid.astype(F32)).astype(BF16)
    return hi, mid, lo


def _dot01_left(m01, v):
    hi, mid, lo = _split3(v)
    m = m01.astype(BF16)
    d = lambda p: jnp.dot(m, p, preferred_element_type=F32)
    return d(hi) + d(mid) + d(lo)


def _dot01_right(v, m01):
    hi, mid, lo = _split3(v)
    m = m01.astype(BF16)
    d = lambda p: jnp.dot(p, m, preferred_element_type=F32)
    return d(hi) + d(mid) + d(lo)


def _dot_nt_precise(a, b):
    ah = a.astype(BF16)
    al = (a - ah.astype(F32)).astype(BF16)
    bh = b.astype(BF16)
    bl = (b - bh.astype(F32)).astype(BF16)
    d = lambda p, q: lax.dot_general(p, q, (((1,), (1,)), ((), ())), preferred_element_type=F32)
    return d(ah, bh) + d(ah, bl) + d(al, bh)


def _sigmoid(v):
    return 1.0 / (1.0 + jnp.exp(-v))


def _silu(v):
    return v * _sigmoid(v)


def _softplus(v):
    return jnp.maximum(v, 0.0) + jnp.log(1.0 + jnp.exp(-jnp.abs(v)))


def _rms_mod(x, g, shift, scale):
    ms = jnp.mean(x * x, axis=-1, keepdims=True)
    y = x * lax.rsqrt(ms + EPS) * g
    return y * (1.0 + scale) + shift


def _ada_kernel(c_ref, w_ref, b_ref, o_ref):
    o_ref[...] = _dot(_silu(c_ref[...]), w_ref[...]) + b_ref[...]


def _ada_mods(c_all, w_ada, b_ada):
    depth, d, d6 = w_ada.shape
    n_mod = d6 // d
    rows = c_all.shape[0]
    out = pl.pallas_call(
        _ada_kernel,
        grid=(depth, n_mod),
        in_specs=[
            pl.BlockSpec((rows, d), lambda i, n: (0, 0)),
            pl.BlockSpec((None, d, d), lambda i, n: (i, 0, n)),
            pl.BlockSpec((None, None, 1, d), lambda i, n: (i, n, 0, 0)),
        ],
        out_specs=pl.BlockSpec((None, None, rows, d), lambda i, n: (i, n, 0, 0)),
        out_shape=jax.ShapeDtypeStruct((depth, n_mod, rows, d), F32),
        compiler_params=_cparams(2, 32),
        name="ada_mods",
    )(c_all, w_ada, b_ada.reshape(depth, n_mod, 1, d))
    return out.transpose(0, 2, 1, 3)


def _moe_pre(x_new, mod, g2, wr_t, b_r, h2_ref, topi_ref, gate_ref, rank_ref, cnt_ref, cnt_scr):
    @pl.when(jnp.logical_and(pl.program_id(0) == 0, pl.program_id(1) == 0))
    def _():
        cnt_scr[...] = jnp.zeros_like(cnt_scr)

    h2 = _rms_mod(x_new, g2, mod[3:4], mod[4:5])
    h2_ref[...] = h2.astype(h2_ref.dtype)
    logits = _dot_nt_precise(wr_t, h2) + b_r
    n_exp = logits.shape[0]
    eidx = lax.broadcasted_iota(jnp.int32, logits.shape, 0)
    vals, idxs = [], []
    cur = logits
    for _ in range(TOP_K):
        m = jnp.max(cur, axis=0, keepdims=True)
        idx = jnp.min(jnp.where(cur == m, eidx, n_exp), axis=0, keepdims=True)
        vals.append(m)
        idxs.append(idx)
        cur = jnp.where(eidx == idx, -jnp.inf, cur)
    es = [jnp.exp(v - vals[0]) for v in vals]
    tot = es[0] + es[1] + es[2] + es[3]
    rows = logits.shape[1]
    t_src = lax.broadcasted_iota(jnp.int32, (rows, rows), 0)
    t_dst = lax.broadcasted_iota(jnp.int32, (rows, rows), 1)
    earlier = jnp.where(t_src < t_dst, 1.0, 0.0).astype(BF16)
    base = cnt_scr[:, 0:1]
    onehots = [jnp.where(eidx == idxs[k], 1.0, 0.0) for k in range(TOP_K)]
    chosen = onehots[0] + onehots[1] + onehots[2] + onehots[3]
    before = base + jnp.dot(chosen.astype(BF16), earlier, preferred_element_type=F32)
    for k in range(TOP_K):
        topi_ref[k:k + 1, :] = idxs[k]
        gate_ref[k:k + 1, :] = es[k] / tot
        rank_ref[k:k + 1, :] = jnp.sum(onehots[k] * before, axis=0, keepdims=True).astype(jnp.int32)
    totals = jnp.broadcast_to(base + jnp.sum(chosen, axis=1, keepdims=True), cnt_scr.shape)
    cnt_scr[...] = totals
    cnt_ref[...] = totals


def _moe_pre_specs(d, n_exp):
    in_specs = [_full_spec((1, d)), _full_spec((n_exp, d)), _full_spec((n_exp, 1))]
    out_specs = [
        pl.BlockSpec((None, TILE, d), lambda b, t: (b, t, 0)),
        pl.BlockSpec((None, TILE, d), lambda b, t: (b, t, 0)),
        pl.BlockSpec((None, TOP_K, TILE), lambda b, t: (b, 0, t)),
        pl.BlockSpec((None, TOP_K, TILE), lambda b, t: (b, 0, t)),
        pl.BlockSpec((None, TOP_K, TILE), lambda b, t: (b, 0, t)),
        _full_spec((n_exp, CNT_LANES)),
    ]
    return in_specs, out_specs


def _moe_pre_shapes(bsz, s_out, d, n_exp):
    return [
        jax.ShapeDtypeStruct((bsz, s_out, d), F32),
        jax.ShapeDtypeStruct((bsz, s_out, d), ACT_DTYPE),
        jax.ShapeDtypeStruct((bsz, TOP_K, s_out), jnp.int32),
        jax.ShapeDtypeStruct((bsz, TOP_K, s_out), F32),
        jax.ShapeDtypeStruct((bsz, TOP_K, s_out), jnp.int32),
        jax.ShapeDtypeStruct((n_exp, CNT_LANES), F32),
    ]


def _moe_pre_scratch(n_exp):
    return pltpu.VMEM((n_exp, CNT_LANES), F32)


def _inproj_kernel(x_ref, mod_ref, g_ref, wz_ref, wx_ref, wdt_ref, wdt_t_ref, dtb_ref, dtb_t_ref,
                   a_ref, a_t_ref, z_ref, xbc_ref, dt_ref, acum_ref, acum_t_ref):
    mod = mod_ref[...]
    h = _rms_mod(x_ref[...], g_ref[...], mod[0:1], mod[1:2]).astype(MXU_DTYPE)
    for g in range(N_SSD_GROUPS):
        z_ref[g] = _dot(h, wz_ref[:, g * GROUP_W:(g + 1) * GROUP_W]).astype(z_ref.dtype)
    for n in range(0, D_XBC, N_CHUNK):
        xbc_ref[:, n:n + N_CHUNK] = _dot(h, wx_ref[:, n:n + N_CHUNK]).astype(xbc_ref.dtype)
    dt = _softplus(_dot(h, wdt_ref[...]) + dtb_ref[...])
    dt_t = _softplus(_dot_nt(wdt_t_ref[...], h) + dtb_t_ref[...])
    dt_ref[...] = dt
    dta = dt * a_ref[...]
    dta_t = dt_t * a_t_ref[...]
    ii = lax.broadcasted_iota(jnp.int32, (TILE, TILE), 0)
    jj = lax.broadcasted_iota(jnp.int32, (TILE, TILE), 1)
    same = (ii // SSD_CHUNK) == (jj // SSD_CHUNK)
    lower = jnp.where(same, jnp.where(jj <= ii, 1.0, 0.0), 0.0)
    upper = jnp.where(same, jnp.where(jj >= ii, 1.0, 0.0), 0.0)
    col = lax.broadcasted_iota(jnp.int32, dta.shape, 1)
    acum_ref[...] = jnp.where(col < N_SSD_HEADS, _dot01_left(lower, dta), _dot01_left(upper, dta))
    row = lax.broadcasted_iota(jnp.int32, dta_t.shape, 0)
    acum_t = jnp.where(row < N_SSD_HEADS, _dot01_right(dta_t, upper), _dot01_right(dta_t, lower))
    acum_t_ref[...] = acum_t - jnp.log(dt_t)


def _ssd_inproj(xs, mod_i, g1, w_in, dt_bias, a_log):
    bsz, s, d = xs.shape
    nt = s // TILE
    ctx_row = bsz
    wz = w_in[:, :D_INNER].astype(MXU_DTYPE)
    wx = w_in[:, D_INNER:D_INNER + D_XBC].astype(MXU_DTYPE)
    wdt = w_in[:, D_INNER + D_XBC:].astype(MXU_DTYPE)
    n_dt = 2 * N_SSD_HEADS
    a = -jnp.exp(a_log.astype(F32)).reshape(1, n_dt)
    dtb = dt_bias.astype(F32).reshape(1, n_dt)
    tok = lambda w: pl.BlockSpec((None, TILE, w), lambda b, t: (b, t, 0))
    return pl.pallas_call(
        _inproj_kernel,
        grid=(bsz, nt),
        in_specs=[
            tok(d),
            pl.BlockSpec((None, 6, d), lambda b, t: (jnp.where(t == 0, ctx_row, b), 0, 0)),
            _full_spec((1, d)),
            _full_spec((d, D_INNER)), _full_spec((d, D_XBC)), _full_spec((d, n_dt)), _full_spec((n_dt, d)),
            _full_spec((1, n_dt)), _full_spec((n_dt, 1)), _full_spec((1, n_dt)), _full_spec((n_dt, 1)),
        ],
        out_specs=[
            pl.BlockSpec((None, N_SSD_GROUPS, TILE, GROUP_W), lambda b, t: (b, 0, t, 0)),
            tok(D_XBC), tok(n_dt), tok(n_dt),
            pl.BlockSpec((None, n_dt, TILE), lambda b, t: (b, 0, t)),
        ],
        out_shape=[
            jax.ShapeDtypeStruct((bsz, N_SSD_GROUPS, s, GROUP_W), ACT_DTYPE),
            jax.ShapeDtypeStruct((bsz, s, D_XBC), ACT_DTYPE),
            jax.ShapeDtypeStruct((bsz, s, n_dt), F32),
            jax.ShapeDtypeStruct((bsz, s, n_dt), F32),
            jax.ShapeDtypeStruct((bsz, n_dt, s), F32),
        ],
        compiler_params=_cparams(2, 56),
        name="ssd_inproj",
    )(xs, mod_i, g1.reshape(1, d), wz, wx, wdt, wdt.T, dtb, dtb.T, a, a.T)


def _ssd_conv_kernel(nt, main_ref, prev_ref, next_ref, w_ref, b_ref, xs_ref, bm_ref, cm_ref, buf):
    t = pl.program_id(1)
    width = w_ref.shape[0]
    half = (width - 1) // 2
    prev_ok = jnp.where(t >= 2, 1.0, 0.0)
    next_ok = jnp.where(jnp.logical_and(t >= 1, t < nt - 1), 1.0, 0.0)
    buf[0:SSD_HALO, :] = prev_ref[...].astype(F32) * prev_ok
    buf[SSD_HALO:SSD_HALO + TILE, :] = main_ref[...].astype(F32)
    buf[SSD_HALO + TILE:, :] = next_ref[...].astype(F32) * next_ok
    rows = 64
    for r0 in range(0, TILE, rows):
        for n in range(0, D_XBC, N_CHUNK):
            acc = jnp.zeros((rows, N_CHUNK), F32) + b_ref[:, n:n + N_CHUNK]
            for k in range(width):
                off = SSD_HALO + r0 + k - half
                acc = acc + w_ref[k:k + 1, n:n + N_CHUNK] * buf[off:off + rows, n:n + N_CHUNK]
            out = _silu(acc)
            for c0 in range(n, n + N_CHUNK, D_STATE):
                piece = out[:, c0 - n:c0 - n + D_STATE]
                if c0 < D_INNER:
                    g, o = divmod(c0, GROUP_W)
                    xs_ref[g, r0:r0 + rows, o:o + D_STATE] = piece.astype(xs_ref.dtype)
                elif c0 < D_INNER + D_BC:
                    bm_ref[(c0 - D_INNER) // D_STATE, r0:r0 + rows, :] = piece.astype(bm_ref.dtype)
                else:
                    cm_ref[(c0 - D_INNER - D_BC) // D_STATE, r0:r0 + rows, :] = piece.astype(cm_ref.dtype)


def _ssd_conv(xbc, w_conv, b_conv):
    bsz, s, c = xbc.shape
    nt = s // TILE
    per_tile = TILE // SSD_HALO
    last_halo = s // SSD_HALO - 1
    grp = lambda w: pl.BlockSpec((None, N_SSD_GROUPS, TILE, w), lambda b, t: (b, 0, t, 0))
    return pl.pallas_call(
        functools.partial(_ssd_conv_kernel, nt),
        grid=(bsz, nt),
        in_specs=[
            pl.BlockSpec((None, TILE, c), lambda b, t: (b, t, 0)),
            pl.BlockSpec((None, SSD_HALO, c), lambda b, t: (b, jnp.maximum(t * per_tile - 1, 0), 0)),
            pl.BlockSpec((None, SSD_HALO, c), lambda b, t: (b, jnp.minimum((t + 1) * per_tile, last_halo), 0)),
            _full_spec(w_conv.shape), _full_spec((1, c)),
        ],
        out_specs=[grp(GROUP_W), grp(D_STATE), grp(D_STATE)],
        out_shape=[
            jax.ShapeDtypeStruct((bsz, N_SSD_GROUPS, s, GROUP_W), ACT_DTYPE),
            jax.ShapeDtypeStruct((bsz, N_SSD_GROUPS, s, D_STATE), ACT_DTYPE),
            jax.ShapeDtypeStruct((bsz, N_SSD_GROUPS, s, D_STATE), ACT_DTYPE),
        ],
        scratch_shapes=[pltpu.VMEM((TILE + 2 * SSD_HALO, c), F32)],
        compiler_params=_cparams(2, 48),
        name="ssd_conv",
    )(xbc, xbc, xbc, w_conv.astype(F32), b_conv.astype(F32).reshape(1, c))


def _ssd_scan_kernel(xs_ref, bm_ref, cm_ref, dt_ref, acol_ref, arow_ref, y_ref, st_ref):
    d = pl.program_id(0)
    c = pl.program_id(2)

    @pl.when(c == 0)
    def _():
        st_ref[...] = jnp.zeros_like(st_ref)

    q = SSD_CHUNK
    ii = lax.broadcasted_iota(jnp.int32, (q, q), 0)
    jj = lax.broadcasted_iota(jnp.int32, (q, q), 1)
    mask = (ii - jj) * (1 - 2 * d) >= 0
    head_of_lane = lax.broadcasted_iota(jnp.int32, (1, GROUP_W), 1) // HEAD_DIM

    def expand(v):
        out = v[:, HEADS_PER_GROUP - 1:HEADS_PER_GROUP]
        for r in range(HEADS_PER_GROUP - 2, -1, -1):
            out = jnp.where(head_of_lane == r, v[:, r:r + 1], out)
        return out

    pair_w = 2 * HEAD_DIM
    first_of_pair = lax.broadcasted_iota(jnp.int32, (1, pair_w), 1) < HEAD_DIM
    for g in range(N_SSD_GROUPS):
        hs = slice(g * HEADS_PER_GROUP, (g + 1) * HEADS_PER_GROUP)
        x = xs_ref[g]
        bm = bm_ref[g]
        cm = cm_ref[g]
        cm_f = cm.astype(F32)
        dtc = dt_ref[:, hs]
        acol = acol_ref[:, hs]
        arow = arow_ref[hs, :]
        a_tot = jnp.where(d == 0, acol[q - 1:q, :], acol[0:1, :])
        scores = _dot_nt(cm, bm)
        st = st_ref[g]
        st_m = st.astype(MXU_DTYPE)
        lhs = []
        for r in range(HEADS_PER_GROUP):
            a_i = jnp.broadcast_to(acol[:, r:r + 1], (q, q))
            within = scores * jnp.where(mask, jnp.exp(a_i - arow[r:r + 1, :]), 0.0)
            carried = cm_f * jnp.exp(a_i)
            lhs.append(jnp.concatenate([within.astype(MXU_DTYPE), carried.astype(MXU_DTYPE)], axis=1))
        for p in range(HEADS_PER_GROUP // 2):
            lanes = slice(p * pair_w, (p + 1) * pair_w)
            rhs = jnp.concatenate([x[:, lanes].astype(MXU_DTYPE), st_m[:, lanes]], axis=0)
            y_pair = jnp.where(first_of_pair, _dot(lhs[2 * p], rhs), _dot(lhs[2 * p + 1], rhs))
            y_ref[g, :, lanes] = y_pair.astype(y_ref.dtype)
        xdte = x.astype(F32) * expand(dtc * jnp.exp(a_tot - acol))
        st_ref[g] = st * expand(jnp.exp(a_tot)) + _dot_tn(bm, xdte)


def _ssd_scan(xs_g, bm_g, cm_g, dt_d, acum_d, acum_t_d, n_ctx_chunks):
    bsz, n_grp, s, _ = xs_g.shape
    nc = s // SSD_CHUNK
    last = nc - 1 + n_ctx_chunks

    def chunk(d, c):
        back = jnp.where(c < n_ctx_chunks, n_ctx_chunks - 1 - c, last - c)
        return jnp.where(d == 0, c, back)

    big = lambda w: pl.BlockSpec((None, n_grp, SSD_CHUNK, w), lambda d, b, c: (b, 0, chunk(d, c), 0))
    col = pl.BlockSpec((None, None, SSD_CHUNK, N_SSD_HEADS), lambda d, b, c: (d, b, chunk(d, c), 0))
    return pl.pallas_call(
        _ssd_scan_kernel,
        grid=(2, bsz, nc),
        in_specs=[
            big(GROUP_W), big(D_STATE), big(D_STATE), col, col,
            pl.BlockSpec((None, None, N_SSD_HEADS, SSD_CHUNK), lambda d, b, c: (d, b, 0, chunk(d, c))),
        ],
        out_specs=pl.BlockSpec((None, None, n_grp, SSD_CHUNK, GROUP_W),
                               lambda d, b, c: (d, b, 0, chunk(d, c), 0)),
        out_shape=jax.ShapeDtypeStruct((2, bsz, n_grp, s, GROUP_W), ACT_DTYPE),
        scratch_shapes=[pltpu.VMEM((n_grp, D_STATE, GROUP_W), F32)],
        compiler_params=_cparams(3, 32),
        name="ssd_scan",
    )(xs_g, bm_g, cm_g, dt_d, acum_d, acum_t_d)


def _ssd_out_kernel(x_ref, mod_ref, yf_ref, yb_ref, xs_ref, z_ref, dsk_ref, ng_ref, wo_ref,
                    g2_ref, wr_ref, br_ref, xo_ref, h2_ref, topi_ref, gate_ref, rank_ref, cnt_ref,
                    yn_scr, cnt_scr):
    mod = mod_ref[...]
    for g in range(N_SSD_GROUPS):
        y = yf_ref[g].astype(F32) + yb_ref[g].astype(F32) + dsk_ref[g] * xs_ref[g].astype(F32)
        y = y * _silu(z_ref[g].astype(F32))
        ms = jnp.mean(y * y, axis=-1, keepdims=True)
        yn_scr[:, g * GROUP_W:(g + 1) * GROUP_W] = (y * lax.rsqrt(ms + EPS) * ng_ref[g]).astype(yn_scr.dtype)
    x_new = x_ref[...] + mod[2:3] * _dot(yn_scr[...], wo_ref[...])
    xo_ref[...] = x_new
    _moe_pre(x_new, mod, g2_ref[...], wr_ref[...], br_ref[...], h2_ref, topi_ref, gate_ref, rank_ref, cnt_ref,
             cnt_scr)


def _ssd_out(xs, mod_i, y, xs_g, z_g, d_skip, norm_g, w_out, g2, w_router, b_router):
    bsz, s, d = xs.shape
    nt = s // TILE
    ctx_row = bsz
    n_exp = w_router.shape[1]
    grp = lambda: pl.BlockSpec((None, N_SSD_GROUPS, TILE, GROUP_W), lambda b, t: (b, 0, t, 0))
    ydir = lambda dd: pl.BlockSpec((None, None, N_SSD_GROUPS, TILE, GROUP_W), lambda b, t: (dd, b, 0, t, 0))
    pre_in, pre_out = _moe_pre_specs(d, n_exp)
    dsk = jnp.repeat(d_skip.astype(F32), HEAD_DIM).reshape(N_SSD_GROUPS, 1, GROUP_W)
    return pl.pallas_call(
        _ssd_out_kernel,
        grid=(bsz, nt),
        in_specs=[
            pl.BlockSpec((None, TILE, d), lambda b, t: (b, t, 0)),
            pl.BlockSpec((None, 6, d), lambda b, t: (jnp.where(t == 0, ctx_row, b), 0, 0)),
            ydir(0), ydir(1), grp(), grp(),
            _full_spec((N_SSD_GROUPS, 1, GROUP_W)), _full_spec((N_SSD_GROUPS, 1, GROUP_W)),
            _full_spec((D_INNER, d)),
        ] + pre_in,
        out_specs=pre_out,
        out_shape=_moe_pre_shapes(bsz, s, d, n_exp),
        scratch_shapes=[pltpu.VMEM((TILE, D_INNER), MXU_DTYPE), _moe_pre_scratch(n_exp)],
        compiler_params=_cparams(2, 48),
        name="ssd_out",
    )(xs, mod_i, y, y, xs_g, z_g, dsk, norm_g.astype(F32).reshape(N_SSD_GROUPS, 1, GROUP_W),
      w_out.astype(MXU_DTYPE), g2.reshape(1, d), w_router.T.astype(F32), b_router.astype(F32).reshape(n_exp, 1))


def _cv_kernel(t0, x_ref, mod_ref, g1_ref, w1_ref, b1_ref, wdw_ref, bdw_ref, lng_ref, lnb_ref, w2_ref, b2_ref,
               g2_ref, wr_ref, br_ref, xo_ref, h2_ref, topi_ref, gate_ref, rank_ref, cnt_ref,
               pad_scr, sh_scr, cv_scr, cnt_scr):
    tile = pl.program_id(1) + t0
    d = x_ref.shape[-1]
    x = x_ref[...]
    mod = mod_ref[...]
    h = _rms_mod(x, g1_ref[...], mod[0:1], mod[1:2]).astype(MXU_DTYPE)
    width = wdw_ref.shape[0]
    half = (width - 1) // 2
    seg = GRID_W
    n_seg = TILE // seg
    stride = seg + 2 * CONV_PAD
    joined = jnp.where(tile == 0, 1.0, 0.0)
    zeros = jnp.zeros((CONV_PAD, d), F32)
    pad_scr[0:CONV_PAD, :] = zeros
    pad_scr[n_seg * stride - CONV_PAD:n_seg * stride, :] = zeros
    for n in range(0, d, N_CHUNK):
        a = _dot(h, w1_ref[:, n:n + N_CHUNK]) + b1_ref[:, n:n + N_CHUNK]
        gate = _dot(h, w1_ref[:, d + n:d + n + N_CHUNK]) + b1_ref[:, d + n:d + n + N_CHUNK]
        u = a * _sigmoid(gate)
        for s_i in range(n_seg):
            base = s_i * stride
            pad_scr[base + CONV_PAD:base + CONV_PAD + seg, n:n + N_CHUNK] = u[s_i * seg:(s_i + 1) * seg]
            if s_i > 0:
                pad_scr[base:base + CONV_PAD, n:n + N_CHUNK] = u[s_i * seg - CONV_PAD:s_i * seg] * joined
            if s_i < n_seg - 1:
                pad_scr[base + CONV_PAD + seg:base + stride, n:n + N_CHUNK] = (
                    u[(s_i + 1) * seg:(s_i + 1) * seg + CONV_PAD] * joined)
    lanes = 256
    pad_rows = n_seg * stride
    for n in range(0, d, lanes):
        blk = pad_scr[:, n:n + lanes]
        for s in range(1, SUBLANES):
            sh_scr[s - 1, :, n:n + lanes] = pltpu.roll(blk, pad_rows - s, axis=0)
    for s_i in range(n_seg):
        for n in range(0, d, lanes):
            acc = jnp.zeros((seg, lanes), F32) + bdw_ref[:, n:n + lanes]
            for k in range(width):
                whole, phase = divmod(CONV_PAD + k - half, SUBLANES)
                row0 = s_i * stride + whole * SUBLANES
                if phase == 0:
                    src = pad_scr[row0:row0 + seg, n:n + lanes]
                else:
                    src = sh_scr[phase - 1, row0:row0 + seg, n:n + lanes]
                acc = acc + wdw_ref[k:k + 1, n:n + lanes] * src
            cv_scr[s_i * seg:(s_i + 1) * seg, n:n + lanes] = acc
    cv = cv_scr[...]
    mu = jnp.mean(cv, axis=-1, keepdims=True)
    xc = cv - mu
    var = jnp.mean(xc * xc, axis=-1, keepdims=True)
    ln = xc * lax.rsqrt(var + EPS) * lng_ref[...] + lnb_ref[...]
    out = _dot(_silu(ln), w2_ref[...]) + b2_ref[...]
    x_new = x + mod[2:3] * out
    xo_ref[...] = x_new
    _moe_pre(x_new, mod, g2_ref[...], wr_ref[...], br_ref[...], h2_ref, topi_ref, gate_ref, rank_ref, cnt_ref,
             cnt_scr)


def _conformer(xs, mod_i, g1, w_pw1, b_pw1, w_dw, b_dw, ln_g, ln_b, w_pw2, b_pw2, g2, w_router, b_router,
               with_ctx):
    bsz, s, d = xs.shape
    t0 = 0 if with_ctx else 1
    nt = s // TILE - t0
    ctx_row = bsz
    n_exp = w_router.shape[1]
    pad_rows = (TILE // GRID_W) * (GRID_W + 2 * CONV_PAD)
    pre_in, pre_out = _moe_pre_specs(d, n_exp)
    row = lambda v: v.astype(F32).reshape(1, -1)
    return pl.pallas_call(
        functools.partial(_cv_kernel, t0),
        grid=(bsz, nt),
        in_specs=[
            pl.BlockSpec((None, TILE, d), lambda b, t: (b, t + t0, 0)),
            pl.BlockSpec((None, 6, d), lambda b, t: (jnp.where(t + t0 == 0, ctx_row, b), 0, 0)),
            _full_spec((1, d)),
            _full_spec((d, 2 * d)), _full_spec((1, 2 * d)),
            _full_spec(w_dw.shape), _full_spec((1, d)), _full_spec((1, d)), _full_spec((1, d)),
            _full_spec((d, d)), _full_spec((1, d)),
        ] + pre_in,
        out_specs=pre_out,
        out_shape=_moe_pre_shapes(bsz, nt * TILE, d, n_exp),
        scratch_shapes=[
            pltpu.VMEM((pad_rows, d), F32),
            pltpu.VMEM((SUBLANES - 1, pad_rows, d), F32),
            pltpu.VMEM((TILE, d), F32),
            _moe_pre_scratch(n_exp),
        ],
        compiler_params=_cparams(2, 56),
        name="conformer",
    )(xs, mod_i, row(g1), w_pw1.astype(MXU_DTYPE), row(b_pw1), w_dw.astype(F32), row(b_dw), row(ln_g), row(ln_b),
      w_pw2.astype(MXU_DTYPE), row(b_pw2), row(g2), w_router.T.astype(F32),
      b_router.astype(F32).reshape(n_exp, 1))


def _moe_kernel(be_ref, flag_ref, x_ref, wgu_ref, bgu_ref, wdn_ref, bdn_ref, y_ref, wgu_scr, wdn_scr, act_scr):
    i = pl.program_id(0)
    flags = flag_ref[i]
    d_ff = wdn_ref.shape[0]

    @pl.when((flags & 2) != 0)
    def _():
        for n in range(0, 2 * d_ff, N_CHUNK):
            wgu_scr[:, n:n + N_CHUNK] = wgu_ref[:, n:n + N_CHUNK].astype(wgu_scr.dtype)
        for n in range(0, wdn_ref.shape[1], N_CHUNK):
            wdn_scr[:, n:n + N_CHUNK] = wdn_ref[:, n:n + N_CHUNK].astype(wdn_scr.dtype)

    @pl.when((flags & 1) != 0)
    def _():
        x = x_ref[...]
        for n in range(0, d_ff, N_CHUNK):
            gate = _dot(x, wgu_scr[:, n:n + N_CHUNK]) + bgu_ref[:, n:n + N_CHUNK]
            up = _dot(x, wgu_scr[:, d_ff + n:d_ff + n + N_CHUNK]) + bgu_ref[:, d_ff + n:d_ff + n + N_CHUNK]
            gate = jnp.minimum(gate, SWIGLU_LIMIT)
            up = jnp.clip(up, -SWIGLU_LIMIT, SWIGLU_LIMIT)
            glu = gate * _sigmoid(gate * SWIGLU_ALPHA)
            act_scr[:, n:n + N_CHUNK] = ((up + 1.0) * glu).astype(act_scr.dtype)
        act = act_scr[...]
        for n in range(0, y_ref.shape[1], N_CHUNK):
            y_ref[:, n:n + N_CHUNK] = (_dot(act, wdn_scr[:, n:n + N_CHUNK])
                                       + bdn_ref[:, n:n + N_CHUNK]).astype(y_ref.dtype)

    @pl.when((flags & 1) == 0)
    def _():
        y_ref[...] = jnp.zeros_like(y_ref)


def _moe_experts(xg, block_expert, block_flags, layer, w_gu, b_gu, w_down, b_down):
    n_rows, d = xg.shape
    depth, n_exp, _, d_gu = w_gu.shape
    d_ff = w_down.shape[2]
    nb = n_rows // MOE_TM
    grid_spec = pltpu.PrefetchScalarGridSpec(
        num_scalar_prefetch=2,
        grid=(nb,),
        in_specs=[
            pl.BlockSpec((MOE_TM, d), lambda i, be, fl: (i, 0)),
            pl.BlockSpec((None, None, d, d_gu), lambda i, be, fl: (layer, be[i], 0, 0)),
            pl.BlockSpec((None, None, 1, d_gu), lambda i, be, fl: (layer, be[i], 0, 0)),
            pl.BlockSpec((None, None, d_ff, d), lambda i, be, fl: (layer, be[i], 0, 0)),
            pl.BlockSpec((None, None, 1, d), lambda i, be, fl: (layer, be[i], 0, 0)),
        ],
        out_specs=pl.BlockSpec((MOE_TM, d), lambda i, be, fl: (i, 0)),
        scratch_shapes=[
            pltpu.VMEM((d, d_gu), MXU_DTYPE),
            pltpu.VMEM((d_ff, d), MXU_DTYPE),
            pltpu.VMEM((MOE_TM, d_ff), MXU_DTYPE),
        ],
    )
    return pl.pallas_call(
        _moe_kernel,
        grid_spec=grid_spec,
        out_shape=jax.ShapeDtypeStruct((n_rows, d), ACT_DTYPE),
        compiler_params=_cparams(1, 56),
        name="moe_experts",
    )(block_expert, block_flags, xg, w_gu, b_gu.reshape(depth, n_exp, 1, d_gu), w_down,
      b_down.reshape(depth, n_exp, 1, d))


def _take(rows, idx):
    return rows.at[idx].get(mode="promise_in_bounds")


def _route(topi, rank, counts):
    n_exp = counts.shape[0]
    n_assign = topi.size
    experts = jnp.arange(n_exp, dtype=jnp.int32)
    padded = (counts + MOE_TM - 1) // MOE_TM * MOE_TM
    pad_end = jnp.cumsum(padded)
    pad_start = pad_end - padded
    unpad_start = jnp.cumsum(counts) - counts
    start_of = jnp.sum(jnp.where(topi[..., None] == experts, pad_start, 0), axis=-1)
    slot_of_assign = start_of + rank
    nb = -(-n_assign // MOE_TM) + n_exp
    block_start = jnp.arange(nb, dtype=jnp.int32) * MOE_TM
    valid = block_start < pad_end[-1]
    be = jnp.minimum(jnp.sum((block_start[:, None] >= pad_end[None, :]).astype(jnp.int32), axis=1), n_exp - 1)
    n_valid = pad_end[-1] // MOE_TM
    be = jnp.where(valid, be, be[jnp.maximum(n_valid - 1, 0)])
    first = jnp.logical_and(valid, block_start == pad_start[be])
    flags = valid.astype(jnp.int32) + 2 * first.astype(jnp.int32)
    by_slot = jnp.argsort(slot_of_assign.reshape(-1)).astype(jnp.int32)
    n_tok = topi.shape[1]
    slot = jnp.arange(nb * MOE_TM, dtype=jnp.int32)
    per_slot = lambda per_block: jnp.repeat(per_block, MOE_TM)
    r_slot = slot - per_slot(pad_start[be])
    occupied = jnp.logical_and(per_slot(valid), r_slot < per_slot(counts[be]))
    src = jnp.clip(per_slot(unpad_start[be]) + r_slot, 0, n_assign - 1)
    tok_of_slot = jnp.where(occupied, _take(by_slot, src) % n_tok, slot % n_tok)
    return tok_of_slot, slot_of_assign, be, flags


def _combine_kernel(final, x_ref, mod_ref, g_ref, p0_ref, p1_ref, p2_ref, p3_ref, fg_ref, o_ref):
    g = g_ref[...]
    f = g[:, 0:1] * p0_ref[...].astype(F32)
    for k, p_ref in ((1, p1_ref), (2, p2_ref), (3, p3_ref)):
        f = f + g[:, k:k + 1] * p_ref[...].astype(F32)
    x = x_ref[...] + mod_ref[5:6, :] * f
    if final:
        ms = jnp.mean(x * x, axis=-1, keepdims=True)
        x = x * lax.rsqrt(ms + EPS) * fg_ref[...]
    o_ref[...] = x


def _combine(x_mid, mod_i, gates_t, picked, final_g, with_ctx, final):
    bsz, s, d = x_mid.shape
    nt = s // TILE
    ctx_row = bsz
    mod_row = (lambda b, t: jnp.where(t == 0, ctx_row, b)) if with_ctx else (lambda b, t: b)
    tok = pl.BlockSpec((None, TILE, d), lambda b, t: (b, t, 0))
    flat = pl.BlockSpec((TILE, d), lambda b, t: (b * nt + t, 0))
    return pl.pallas_call(
        functools.partial(_combine_kernel, final),
        grid=(bsz, nt),
        in_specs=[
            tok,
            pl.BlockSpec((None, 6, d), lambda b, t: (mod_row(b, t), 0, 0)),
            pl.BlockSpec((None, TILE, TOP_K), lambda b, t: (b, t, 0)),
            flat, flat, flat, flat,
            _full_spec((1, d)),
        ],
        out_specs=tok,
        out_shape=jax.ShapeDtypeStruct((bsz, s, d), F32),
        compiler_params=_cparams(2, 32),
        name="moe_combine",
    )(x_mid, mod_i, gates_t, *picked, final_g.astype(F32).reshape(1, d))


def _moe(x_mid, mod_i, h2, topi, gates, rank, counts, layer, w_gu, b_gu, w_down, b_down, final_g, with_ctx, final):
    bsz, s, d = h2.shape
    flat = lambda v: v.transpose(1, 0, 2).reshape(TOP_K, bsz * s)
    tok_of_slot, slot_of_assign, be, flags = _route(flat(topi), flat(rank), counts[:, 0].astype(jnp.int32))
    xg = _take(h2.reshape(bsz * s, d), tok_of_slot)
    y = _moe_experts(xg, be, flags, layer, w_gu, b_gu, w_down, b_down)
    picked = [_take(y, slot_of_assign[k]) for k in range(TOP_K)]
    return _combine(x_mid, mod_i, gates.transpose(0, 2, 1), picked, final_g, with_ctx, final)


def kernel(x, c, ctx, c_ctx, w_ada, b_ada, norm1_g, norm2_g, ssd_w_in, ssd_w_conv, ssd_b_conv, ssd_dt_bias, ssd_a_log, ssd_d, ssd_norm_g, ssd_w_out, cv_w_pw1, cv_b_pw1, cv_w_dw, cv_b_dw, cv_ln_g, cv_ln_b, cv_w_pw2, cv_b_pw2, moe_w_router, moe_b_router, moe_w_gu, moe_b_gu, moe_w_down, moe_b_down, final_g):
    bsz, seq, d = x.shape
    n_ctx = ctx.shape[1]
    depth = w_ada.shape[0]
    assert n_ctx == TILE and seq % TILE == 0 and TILE % GRID_W == 0 and TILE % SSD_CHUNK == 0
    assert depth % N_MIXERS == 0

    mod_rows = -(-(bsz + 1) // 8) * 8
    c_all = jnp.zeros((mod_rows, d), F32).at[:bsz].set(c).at[bsz].set(c_ctx)
    mods = _ada_mods(c_all, w_ada, b_ada)
    xs = jnp.concatenate([ctx, x], axis=1)

    for i in range(depth):
        j = i // N_MIXERS
        use_ssd = (i % N_MIXERS) == 0
        need_ctx = i < depth - 1
        mod_i = mods[i]
        if use_ssd:
            z_g, xbc, dt, acum, acum_t = _ssd_inproj(xs, mod_i, norm1_g[i], ssd_w_in[j], ssd_dt_bias[j],
                                                     ssd_a_log[j])
            xs_g, bm_g, cm_g = _ssd_conv(xbc, ssd_w_conv[j], ssd_b_conv[j])
            s = xs.shape[1]
            by_dir = lambda v: v.reshape(bsz, s, 2, N_SSD_HEADS).transpose(2, 0, 1, 3)
            acum_t_d = acum_t.reshape(bsz, 2, N_SSD_HEADS, s).transpose(1, 0, 2, 3)
            y = _ssd_scan(xs_g, bm_g, cm_g, by_dir(dt), by_dir(acum), acum_t_d, n_ctx // SSD_CHUNK)
            x_mid, h2, topi, gates, rank, counts = _ssd_out(
                xs, mod_i, y, xs_g, z_g, ssd_d[j], ssd_norm_g[j], ssd_w_out[j], norm2_g[i], moe_w_router[i],
                moe_b_router[i])
        else:
            x_mid, h2, topi, gates, rank, counts = _conformer(
                xs, mod_i, norm1_g[i], cv_w_pw1[j], cv_b_pw1[j], cv_w_dw[j], cv_b_dw[j], cv_ln_g[j], cv_ln_b[j],
                cv_w_pw2[j], cv_b_pw2[j], norm2_g[i], moe_w_router[i], moe_b_router[i], need_ctx)
        xs = _moe(x_mid, mod_i, h2, topi, gates, rank, counts, i, moe_w_gu, moe_b_gu, moe_w_down, moe_b_down,
                  final_g, need_ctx, i == depth - 1)
    return xs
```

```python
import functools

import jax
import jax.numpy as jnp
from jax import lax
from jax.experimental import pallas as pl
from jax.experimental.pallas import tpu as pltpu

F32 = jnp.float32
BF16 = jnp.bfloat16
MXU_DTYPE = BF16
ACT_DTYPE = BF16

EPS = 1e-6
LOG2_E = 1.4426950408889634
GRID_W = 64
N_MIXERS = 2
HEAD_DIM = 64
N_SSD_HEADS = 32
N_SSD_GROUPS = 8
HEADS_PER_GROUP = N_SSD_HEADS // N_SSD_GROUPS
D_STATE = 128
SSD_CHUNK = 128
GROUP_W = HEADS_PER_GROUP * HEAD_DIM
D_INNER = N_SSD_HEADS * HEAD_DIM
D_BC = N_SSD_GROUPS * D_STATE
D_XBC = D_INNER + 2 * D_BC
TOP_K = 4
SWIGLU_ALPHA = 1.702
SWIGLU_LIMIT = 7.0

TILE = 256
CONV_PAD = 16
SSD_HALO = 8
N_CHUNK = 512
MOE_TM = 512
CNT_LANES = 128
SUBLANES = 8
VMEM_MB = 1024 * 1024


def _cparams(n_axes, vmem_mb):
    return pltpu.CompilerParams(dimension_semantics=("arbitrary",) * n_axes,
                                vmem_limit_bytes=vmem_mb * VMEM_MB)


def _full_spec(shape):
    zeros = (0,) * len(shape)
    return pl.BlockSpec(shape, lambda *_: zeros)


def _dot(a, b):
    return jnp.dot(a.astype(MXU_DTYPE), b.astype(MXU_DTYPE), preferred_element_type=F32)


def _dot_nt(a, b):
    return lax.dot_general(a.astype(MXU_DTYPE), b.astype(MXU_DTYPE), (((1,), (1,)), ((), ())),
                           preferred_element_type=F32)


def _dot_tn(a, b):
    return lax.dot_general(a.astype(MXU_DTYPE), b.astype(MXU_DTYPE), (((0,), (0,)), ((), ())),
                           preferred_element_type=F32)


def _split3(v):
    hi = v.astype(BF16)
    r1 = v - hi.astype(F32)
    mid = r1.astype(BF16)
    lo = (r1 - mid.astype(F32)).astype(BF16)
    return hi, mid, lo


def _dot01_left(m01, v):
    hi, mid, lo = _split3(v)
    m = m01.astype(BF16)
    d = lambda p: jnp.dot(m, p, preferred_element_type=F32)
    return d(hi) + d(mid) + d(lo)


def _dot01_right(v, m01):
    hi, mid, lo = _split3(v)
    m = m01.astype(BF16)
    d = lambda p: jnp.dot(p, m, preferred_element_type=F32)
    return d(hi) + d(mid) + d(lo)


def _dot_nt_precise(a, b):
    ah = a.astype(BF16)
    al = (a - ah.astype(F32)).astype(BF16)
    bh = b.astype(BF16)
    bl = (b - bh.astype(F32)).astype(BF16)
    d = lambda p, q: lax.dot_general(p, q, (((1,), (1,)), ((), ())), preferred_element_type=F32)
    return d(ah, bh) + d(ah, bl) + d(al, bh)


def _sigmoid(v):
    return 1.0 / (1.0 + jnp.exp(-v))


def _silu(v):
    return v * _sigmoid(v)


def _softplus(v):
    return jnp.maximum(v, 0.0) + jnp.log(1.0 + jnp.exp(-jnp.abs(v)))


def _rms_mod(x, g, shift, scale):
    ms = jnp.mean(x * x, axis=-1, keepdims=True)
    y = x * lax.rsqrt(ms + EPS) * g
    return y * (1.0 + scale) + shift


def _ada_kernel(c_ref, w_ref, b_ref, o_ref):
    o_ref[...] = _dot(_silu(c_ref[...]), w_ref[...]) + b_ref[...]


def _ada_mods(c_all, w_ada, b_ada):
    depth, d, d6 = w_ada.shape
    n_mod = d6 // d
    rows = c_all.shape[0]
    out = pl.pallas_call(
        _ada_kernel,
        grid=(depth, n_mod),
        in_specs=[
            pl.BlockSpec((rows, d), lambda i, n: (0, 0)),
            pl.BlockSpec((None, d, d), lambda i, n: (i, 0, n)),
            pl.BlockSpec((None, None, 1, d), lambda i, n: (i, n, 0, 0)),
        ],
        out_specs=pl.BlockSpec((None, None, rows, d), lambda i, n: (i, n, 0, 0)),
        out_shape=jax.ShapeDtypeStruct((depth, n_mod, rows, d), F32),
        compiler_params=_cparams(2, 32),
        name="ada_mods",
    )(c_all, w_ada, b_ada.reshape(depth, n_mod, 1, d))
    return out.transpose(0, 2, 1, 3)


def _moe_pre(x_new, mod, g2, wr_t, b_r, h2_ref, topi_ref, gate_ref, rank_ref, cnt_ref, cnt_scr):
    @pl.when(jnp.logical_and(pl.program_id(0) == 0, pl.program_id(1) == 0))
    def _():
        cnt_scr[...] = jnp.zeros_like(cnt_scr)

    h2 = _rms_mod(x_new, g2, mod[3:4], mod[4:5])
    h2_ref[...] = h2.astype(h2_ref.dtype)
    logits = _dot_nt_precise(wr_t, h2) + b_r
    n_exp = logits.shape[0]
    eidx = lax.broadcasted_iota(jnp.int32, logits.shape, 0)
    vals, idxs = [], []
    cur = logits
    for _ in range(TOP_K):
        m = jnp.max(cur, axis=0, keepdims=True)
        idx = jnp.min(jnp.where(cur == m, eidx, n_exp), axis=0, keepdims=True)
        vals.append(m)
        idxs.append(idx)
        cur = jnp.where(eidx == idx, -jnp.inf, cur)
    es = [jnp.exp(v - vals[0]) for v in vals]
    tot = es[0] + es[1] + es[2] + es[3]
    rows = logits.shape[1]
    t_src = lax.broadcasted_iota(jnp.int32, (rows, rows), 0)
    t_dst = lax.broadcasted_iota(jnp.int32, (rows, rows), 1)
    earlier = jnp.where(t_src < t_dst, 1.0, 0.0).astype(BF16)
    base = cnt_scr[:, 0:1]
    onehots = [jnp.where(eidx == idxs[k], 1.0, 0.0) for k in range(TOP_K)]
    chosen = onehots[0] + onehots[1] + onehots[2] + onehots[3]
    before = base + jnp.dot(chosen.astype(BF16), earlier, preferred_element_type=F32)
    for k in range(TOP_K):
        topi_ref[k:k + 1, :] = idxs[k]
        gate_ref[k:k + 1, :] = es[k] / tot
        rank_ref[k:k + 1, :] = jnp.sum(onehots[k] * before, axis=0, keepdims=True).astype(jnp.int32)
    totals = jnp.broadcast_to(base + jnp.sum(chosen, axis=1, keepdims=True), cnt_scr.shape)
    cnt_scr[...] = totals
    cnt_ref[...] = totals


def _moe_pre_specs(d, n_exp):
    in_specs = [_full_spec((1, d)), _full_spec((n_exp, d)), _full_spec((n_exp, 1))]
    out_specs = [
        pl.BlockSpec((None, TILE, d), lambda b, t: (b, t, 0)),
        pl.BlockSpec((None, TILE, d), lambda b, t: (b, t, 0)),
        pl.BlockSpec((None, TOP_K, TILE), lambda b, t: (b, 0, t)),
        pl.BlockSpec((None, TOP_K, TILE), lambda b, t: (b, 0, t)),
        pl.BlockSpec((None, TOP_K, TILE), lambda b, t: (b, 0, t)),
        _full_spec((n_exp, CNT_LANES)),
    ]
    return in_specs, out_specs


def _moe_pre_shapes(bsz, s_out, d, n_exp):
    return [
        jax.ShapeDtypeStruct((bsz, s_out, d), F32),
        jax.ShapeDtypeStruct((bsz, s_out, d), ACT_DTYPE),
        jax.ShapeDtypeStruct((bsz, TOP_K, s_out), jnp.int32),
        jax.ShapeDtypeStruct((bsz, TOP_K, s_out), F32),
        jax.ShapeDtypeStruct((bsz, TOP_K, s_out), jnp.int32),
        jax.ShapeDtypeStruct((n_exp, CNT_LANES), F32),
    ]


def _moe_pre_scratch(n_exp):
    return pltpu.VMEM((n_exp, CNT_LANES), F32)


def _inproj_kernel(x_ref, mod_ref, g_ref, wz_ref, wx_ref, wdt_ref, wdt_t_ref, dtb_ref, dtb_t_ref,
                   a_ref, a_t_ref, z_ref, xbc_ref, acum_ref, acum_t_ref):
    mod = mod_ref[...]
    h = _rms_mod(x_ref[...], g_ref[...], mod[0:1], mod[1:2]).astype(MXU_DTYPE)
    for g in range(N_SSD_GROUPS):
        z_ref[g] = _dot(h, wz_ref[:, g * GROUP_W:(g + 1) * GROUP_W]).astype(z_ref.dtype)
    for n in range(0, D_XBC, N_CHUNK):
        xbc_ref[:, n:n + N_CHUNK] = _dot(h, wx_ref[:, n:n + N_CHUNK]).astype(xbc_ref.dtype)
    dt = _softplus(_dot(h, wdt_ref[...]) + dtb_ref[...])
    dt_t = _softplus(_dot_nt(wdt_t_ref[...], h) + dtb_t_ref[...])
    dta = dt * a_ref[...]
    dta_t = dt_t * a_t_ref[...]
    ii = lax.broadcasted_iota(jnp.int32, (TILE, TILE), 0)
    jj = lax.broadcasted_iota(jnp.int32, (TILE, TILE), 1)
    same = (ii // SSD_CHUNK) == (jj // SSD_CHUNK)
    lower = jnp.where(same, jnp.where(jj <= ii, 1.0, 0.0), 0.0)
    upper = jnp.where(same, jnp.where(jj >= ii, 1.0, 0.0), 0.0)
    col = lax.broadcasted_iota(jnp.int32, dta.shape, 1)
    acum_ref[...] = jnp.where(col < N_SSD_HEADS, _dot01_left(lower, dta), _dot01_left(upper, dta))
    row = lax.broadcasted_iota(jnp.int32, dta_t.shape, 0)
    acum_t = jnp.where(row < N_SSD_HEADS, _dot01_right(dta_t, upper), _dot01_right(dta_t, lower))
    acum_t_ref[...] = acum_t - jnp.log(dt_t) * LOG2_E


def _ssd_inproj(xs, mod_i, g1, w_in, dt_bias, a_log):
    bsz, s, d = xs.shape
    nt = s // TILE
    ctx_row = bsz
    wz = w_in[:, :D_INNER].astype(MXU_DTYPE)
    wx = w_in[:, D_INNER:D_INNER + D_XBC].astype(MXU_DTYPE)
    wdt = w_in[:, D_INNER + D_XBC:].astype(MXU_DTYPE)
    n_dt = 2 * N_SSD_HEADS
    a = -jnp.exp(a_log.astype(F32)).reshape(1, n_dt) * LOG2_E
    dtb = dt_bias.astype(F32).reshape(1, n_dt)
    tok = lambda w: pl.BlockSpec((None, TILE, w), lambda b, t: (b, t, 0))
    return pl.pallas_call(
        _inproj_kernel,
        grid=(bsz, nt),
        in_specs=[
            tok(d),
            pl.BlockSpec((None, 6, d), lambda b, t: (jnp.where(t == 0, ctx_row, b), 0, 0)),
            _full_spec((1, d)),
            _full_spec((d, D_INNER)), _full_spec((d, D_XBC)), _full_spec((d, n_dt)), _full_spec((n_dt, d)),
            _full_spec((1, n_dt)), _full_spec((n_dt, 1)), _full_spec((1, n_dt)), _full_spec((n_dt, 1)),
        ],
        out_specs=[
            pl.BlockSpec((None, N_SSD_GROUPS, TILE, GROUP_W), lambda b, t: (b, 0, t, 0)),
            tok(D_XBC), tok(n_dt),
            pl.BlockSpec((None, n_dt, TILE), lambda b, t: (b, 0, t)),
        ],
        out_shape=[
            jax.ShapeDtypeStruct((bsz, N_SSD_GROUPS, s, GROUP_W), ACT_DTYPE),
            jax.ShapeDtypeStruct((bsz, s, D_XBC), ACT_DTYPE),
            jax.ShapeDtypeStruct((bsz, s, n_dt), F32),
            jax.ShapeDtypeStruct((bsz, n_dt, s), F32),
        ],
        compiler_params=_cparams(2, 56),
        name="ssd_inproj",
    )(xs, mod_i, g1.reshape(1, d), wz, wx, wdt, wdt.T, dtb, dtb.T, a, a.T)


def _ssd_conv_kernel(nt, main_ref, prev_ref, next_ref, w_ref, b_ref, xs_ref, bmt_ref, cm_ref, shifted_scr, bm_scr):
    t = pl.program_id(1)
    width = w_ref.shape[0]
    half = (width - 1) // 2
    assert half <= SSD_HALO
    prev_ok = jnp.where(t >= 2, 1.0, 0.0)
    next_ok = jnp.where(jnp.logical_and(t >= 1, t < nt - 1), 1.0, 0.0)
    taps = [k for k in range(width) if k != half]
    r_out = lax.broadcasted_iota(jnp.int32, (TILE, TILE), 0)
    r_in = lax.broadcasted_iota(jnp.int32, (TILE, TILE), 1)
    shifts = jnp.concatenate(
        [jnp.where(r_in - r_out == k - half, 1.0, 0.0).astype(main_ref.dtype) for k in taps], axis=0)
    halo_row = lax.broadcasted_iota(jnp.int32, (SSD_HALO, 1), 0)
    rows = 64
    for n in range(0, D_XBC, N_CHUNK):
        cols = slice(n, n + N_CHUNK)
        u = main_ref[:, cols]
        shifted_scr[...] = jnp.dot(shifts, u, preferred_element_type=F32)
        pv = prev_ref[:, cols].astype(F32) * prev_ok
        nx = next_ref[:, cols].astype(F32) * next_ok
        head = jnp.zeros((SSD_HALO, N_CHUNK), F32)
        tail = jnp.zeros((SSD_HALO, N_CHUNK), F32)
        for k in range(width):
            m = abs(k - half)
            if k < half:
                head = head + jnp.where(halo_row < m, w_ref[k:k + 1, cols] * pltpu.roll(pv, m, axis=0), 0.0)
            elif k > half:
                tail = tail + jnp.where(halo_row >= SSD_HALO - m,
                                        w_ref[k:k + 1, cols] * pltpu.roll(nx, SSD_HALO - m, axis=0), 0.0)
        for r0 in range(0, TILE, rows):
            acc = b_ref[:, cols] + w_ref[half:half + 1, cols] * u[r0:r0 + rows].astype(F32)
            for i, k in enumerate(taps):
                acc = acc + w_ref[k:k + 1, cols] * shifted_scr[i * TILE + r0:i * TILE + r0 + rows, :]
            if r0 == 0:
                acc = jnp.concatenate([acc[:SSD_HALO] + head, acc[SSD_HALO:]], axis=0)
            if r0 == TILE - rows:
                acc = jnp.concatenate([acc[:rows - SSD_HALO], acc[rows - SSD_HALO:] + tail], axis=0)
            out = _silu(acc)
            for c0 in range(n, n + N_CHUNK, D_STATE):
                piece = out[:, c0 - n:c0 - n + D_STATE]
                if c0 < D_INNER:
                    g, o = divmod(c0, GROUP_W)
                    xs_ref[g, r0:r0 + rows, o:o + D_STATE] = piece.astype(xs_ref.dtype)
                elif c0 < D_INNER + D_BC:
                    bm_scr[r0:r0 + rows, c0 - D_INNER:c0 - D_INNER + D_STATE] = piece
                else:
                    cm_ref[(c0 - D_INNER - D_BC) // D_STATE, r0:r0 + rows, :] = piece.astype(cm_ref.dtype)
    for g in range(N_SSD_GROUPS):
        for r0 in range(0, TILE, D_STATE):
            blk = bm_scr[r0:r0 + D_STATE, g * D_STATE:(g + 1) * D_STATE]
            bmt_ref[g, :, r0:r0 + D_STATE] = blk.T.astype(bmt_ref.dtype)


def _ssd_conv(xbc, w_conv, b_conv):
    bsz, s, c = xbc.shape
    nt = s // TILE
    per_tile = TILE // SSD_HALO
    last_halo = s // SSD_HALO - 1
    grp = lambda w: pl.BlockSpec((None, N_SSD_GROUPS, TILE, w), lambda b, t: (b, 0, t, 0))
    return pl.pallas_call(
        functools.partial(_ssd_conv_kernel, nt),
        grid=(bsz, nt),
        in_specs=[
            pl.BlockSpec((None, TILE, c), lambda b, t: (b, t, 0)),
            pl.BlockSpec((None, SSD_HALO, c), lambda b, t: (b, jnp.maximum(t * per_tile - 1, 0), 0)),
            pl.BlockSpec((None, SSD_HALO, c), lambda b, t: (b, jnp.minimum((t + 1) * per_tile, last_halo), 0)),
            _full_spec(w_conv.shape), _full_spec((1, c)),
        ],
        out_specs=[
            grp(GROUP_W),
            pl.BlockSpec((None, N_SSD_GROUPS, D_STATE, TILE), lambda b, t: (b, 0, 0, t)),
            grp(D_STATE),
        ],
        out_shape=[
            jax.ShapeDtypeStruct((bsz, N_SSD_GROUPS, s, GROUP_W), ACT_DTYPE),
            jax.ShapeDtypeStruct((bsz, N_SSD_GROUPS, D_STATE, s), ACT_DTYPE),
            jax.ShapeDtypeStruct((bsz, N_SSD_GROUPS, s, D_STATE), ACT_DTYPE),
        ],
        scratch_shapes=[
            pltpu.VMEM(((w_conv.shape[0] - 1) * TILE, N_CHUNK), F32),
            pltpu.VMEM((TILE, D_BC), F32),
        ],
        compiler_params=_cparams(2, 48),
        name="ssd_conv",
    )(xbc, xbc, xbc, w_conv.astype(F32), b_conv.astype(F32).reshape(1, c))


def _ssd_scan_kernel(xs_ref, bmt_ref, cm_ref, acol_ref, arow_ref, y_ref, st_ref):
    d = pl.program_id(0)
    c = pl.program_id(2)

    @pl.when(c == 0)
    def _():
        st_ref[...] = jnp.zeros_like(st_ref)

    q = SSD_CHUNK
    ii = lax.broadcasted_iota(jnp.int32, (q, q), 0)
    jj = lax.broadcasted_iota(jnp.int32, (q, q), 1)
    mask = (ii - jj) * (1 - 2 * d) >= 0
    pair_w = 2 * HEAD_DIM
    first_of_pair = lax.broadcasted_iota(jnp.int32, (1, pair_w), 1) < HEAD_DIM
    def scores_of(g):
        return _dot(cm_ref[g], bmt_ref[g])

    def factors_of(g, scores):
        hs = slice(g * HEADS_PER_GROUP, (g + 1) * HEADS_PER_GROUP)
        bmt_f = bmt_ref[g].astype(F32)
        cm_f = cm_ref[g].astype(F32)
        acol = acol_ref[:, hs]
        arow = arow_ref[hs, :]
        a_tot = jnp.where(d == 0, acol[q - 1:q, :], acol[0:1, :])
        lhs, new = [], []
        for r in range(HEADS_PER_GROUP):
            a_i = jnp.broadcast_to(acol[:, r:r + 1], (q, q))
            within = scores * jnp.where(mask, jnp.exp2(a_i - arow[r:r + 1, :]), 0.0)
            carried = cm_f * jnp.exp2(a_i)
            lhs.append(jnp.concatenate([within.astype(MXU_DTYPE), carried.astype(MXU_DTYPE)], axis=1))
            new.append((bmt_f * jnp.exp2(a_tot[:, r:r + 1] - arow[r:r + 1, :])).astype(MXU_DTYPE))
        return lhs, new, a_tot

    ahead = [scores_of(0)]
    for g in range(N_SSD_GROUPS):
        if g + 1 < N_SSD_GROUPS:
            ahead.append(scores_of(g + 1))
        lhs, new, a_tot = factors_of(g, ahead[g])
        x = xs_ref[g].astype(MXU_DTYPE)
        st = st_ref[g]
        st_m = st.astype(MXU_DTYPE)
        for p in range(HEADS_PER_GROUP // 2):
            lanes = slice(p * pair_w, (p + 1) * pair_w)
            x_pair = x[:, lanes]
            rhs = jnp.concatenate([x_pair, st_m[:, lanes]], axis=0)
            y_pair = jnp.where(first_of_pair, _dot(lhs[2 * p], rhs), _dot(lhs[2 * p + 1], rhs))
            y_ref[g, :, lanes] = y_pair.astype(y_ref.dtype)
            keep = jnp.exp2(jnp.where(first_of_pair, a_tot[:, 2 * p:2 * p + 1], a_tot[:, 2 * p + 1:2 * p + 2]))
            zero = jnp.zeros_like(x_pair)
            x_split = jnp.concatenate([jnp.where(first_of_pair, x_pair, zero),
                                       jnp.where(first_of_pair, zero, x_pair)], axis=0)
            b_both = jnp.concatenate([new[2 * p], new[2 * p + 1]], axis=1)
            st_ref[g, :, lanes] = st[:, lanes] * keep + _dot(b_both, x_split)


def _ssd_scan(xs_g, bmt_g, cm_g, acum_d, acum_t_d, n_ctx_chunks):
    bsz, n_grp, s, _ = xs_g.shape
    nc = s // SSD_CHUNK
    last = nc - 1 + n_ctx_chunks

    def chunk(d, c):
        back = jnp.where(c < n_ctx_chunks, n_ctx_chunks - 1 - c, last - c)
        return jnp.where(d == 0, c, back)

    big = lambda w: pl.BlockSpec((None, n_grp, SSD_CHUNK, w), lambda d, b, c: (b, 0, chunk(d, c), 0))
    col = pl.BlockSpec((None, None, SSD_CHUNK, N_SSD_HEADS), lambda d, b, c: (d, b, chunk(d, c), 0))
    return pl.pallas_call(
        _ssd_scan_kernel,
        grid=(2, bsz, nc),
        in_specs=[
            big(GROUP_W),
            pl.BlockSpec((None, n_grp, D_STATE, SSD_CHUNK), lambda d, b, c: (b, 0, 0, chunk(d, c))),
            big(D_STATE), col,
            pl.BlockSpec((None, None, N_SSD_HEADS, SSD_CHUNK), lambda d, b, c: (d, b, 0, chunk(d, c))),
        ],
        out_specs=pl.BlockSpec((None, None, n_grp, SSD_CHUNK, GROUP_W),
                               lambda d, b, c: (d, b, 0, chunk(d, c), 0)),
        out_shape=jax.ShapeDtypeStruct((2, bsz, n_grp, s, GROUP_W), ACT_DTYPE),
        scratch_shapes=[pltpu.VMEM((n_grp, D_STATE, GROUP_W), F32)],
        compiler_params=_cparams(3, 32),
        name="ssd_scan",
    )(xs_g, bmt_g, cm_g, acum_d, acum_t_d)


def _ssd_out_kernel(x_ref, mod_ref, yf_ref, yb_ref, xs_ref, z_ref, dsk_ref, ng_ref, wo_ref,
                    g2_ref, wr_ref, br_ref, xo_ref, h2_ref, topi_ref, gate_ref, rank_ref, cnt_ref,
                    yn_scr, cnt_scr):
    mod = mod_ref[...]
    for g in range(N_SSD_GROUPS):
        y = yf_ref[g].astype(F32) + yb_ref[g].astype(F32) + dsk_ref[g] * xs_ref[g].astype(F32)
        y = y * _silu(z_ref[g].astype(F32))
        ms = jnp.mean(y * y, axis=-1, keepdims=True)
        yn_scr[:, g * GROUP_W:(g + 1) * GROUP_W] = (y * lax.rsqrt(ms + EPS) * ng_ref[g]).astype(yn_scr.dtype)
    x_new = x_ref[...] + mod[2:3] * _dot(yn_scr[...], wo_ref[...])
    xo_ref[...] = x_new
    _moe_pre(x_new, mod, g2_ref[...], wr_ref[...], br_ref[...], h2_ref, topi_ref, gate_ref, rank_ref, cnt_ref,
             cnt_scr)


def _ssd_out(xs, mod_i, y, xs_g, z_g, d_skip, norm_g, w_out, g2, w_router, b_router):
    bsz, s, d = xs.shape
    nt = s // TILE
    ctx_row = bsz
    n_exp = w_router.shape[1]
    grp = lambda: pl.BlockSpec((None, N_SSD_GROUPS, TILE, GROUP_W), lambda b, t: (b, 0, t, 0))
    ydir = lambda dd: pl.BlockSpec((None, None, N_SSD_GROUPS, TILE, GROUP_W), lambda b, t: (dd, b, 0, t, 0))
    pre_in, pre_out = _moe_pre_specs(d, n_exp)
    dsk = jnp.repeat(d_skip.astype(F32), HEAD_DIM).reshape(N_SSD_GROUPS, 1, GROUP_W)
    return pl.pallas_call(
        _ssd_out_kernel,
        grid=(bsz, nt),
        in_specs=[
            pl.BlockSpec((None, TILE, d), lambda b, t: (b, t, 0)),
            pl.BlockSpec((None, 6, d), lambda b, t: (jnp.where(t == 0, ctx_row, b), 0, 0)),
            ydir(0), ydir(1), grp(), grp(),
            _full_spec((N_SSD_GROUPS, 1, GROUP_W)), _full_spec((N_SSD_GROUPS, 1, GROUP_W)),
            _full_spec((D_INNER, d)),
        ] + pre_in,
        out_specs=pre_out,
        out_shape=_moe_pre_shapes(bsz, s, d, n_exp),
        scratch_shapes=[pltpu.VMEM((TILE, D_INNER), MXU_DTYPE), _moe_pre_scratch(n_exp)],
        compiler_params=_cparams(2, 48),
        name="ssd_out",
    )(xs, mod_i, y, y, xs_g, z_g, dsk, norm_g.astype(F32).reshape(N_SSD_GROUPS, 1, GROUP_W),
      w_out.astype(MXU_DTYPE), g2.reshape(1, d), w_router.T.astype(F32), b_router.astype(F32).reshape(n_exp, 1))


def _cv_kernel(t0, x_ref, mod_ref, g1_ref, w1_ref, b1_ref, wdw_ref, bdw_ref, lng_ref, lnb_ref, w2_ref, b2_ref,
               g2_ref, wr_ref, br_ref, xo_ref, h2_ref, topi_ref, gate_ref, rank_ref, cnt_ref,
               pad_scr, sh_scr, cv_scr, cnt_scr):
    tile = pl.program_id(1) + t0
    d = x_ref.shape[-1]
    x = x_ref[...]
    mod = mod_ref[...]
    h = _rms_mod(x, g1_ref[...], mod[0:1], mod[1:2]).astype(MXU_DTYPE)
    width = wdw_ref.shape[0]
    half = (width - 1) // 2
    seg = GRID_W
    n_seg = TILE // seg
    stride = seg + 2 * CONV_PAD
    joined = jnp.where(tile == 0, 1.0, 0.0)
    zeros = jnp.zeros((CONV_PAD, d), F32)
    pad_scr[0:CONV_PAD, :] = zeros
    pad_scr[n_seg * stride - CONV_PAD:n_seg * stride, :] = zeros
    for n in range(0, d, N_CHUNK):
        a = _dot(h, w1_ref[:, n:n + N_CHUNK]) + b1_ref[:, n:n + N_CHUNK]
        gate = _dot(h, w1_ref[:, d + n:d + n + N_CHUNK]) + b1_ref[:, d + n:d + n + N_CHUNK]
        u = a * _sigmoid(gate)
        for s_i in range(n_seg):
            base = s_i * stride
            pad_scr[base + CONV_PAD:base + CONV_PAD + seg, n:n + N_CHUNK] = u[s_i * seg:(s_i + 1) * seg]
            if s_i > 0:
                pad_scr[base:base + CONV_PAD, n:n + N_CHUNK] = u[s_i * seg - CONV_PAD:s_i * seg] * joined
            if s_i < n_seg - 1:
                pad_scr[base + CONV_PAD + seg:base + stride, n:n + N_CHUNK] = (
                    u[(s_i + 1) * seg:(s_i + 1) * seg + CONV_PAD] * joined)
    lanes = 256
    pad_rows = n_seg * stride
    for n in range(0, d, lanes):
        blk = pad_scr[:, n:n + lanes]
        for s in range(1, SUBLANES):
            sh_scr[s - 1, :, n:n + lanes] = pltpu.roll(blk, pad_rows - s, axis=0)
    for s_i in range(n_seg):
        for n in range(0, d, lanes):
            acc = jnp.zeros((seg, lanes), F32) + bdw_ref[:, n:n + lanes]
            for k in range(width):
                whole, phase = divmod(CONV_PAD + k - half, SUBLANES)
                row0 = s_i * stride + whole * SUBLANES
                if phase == 0:
                    src = pad_scr[row0:row0 + seg, n:n + lanes]
                else:
                    src = sh_scr[phase - 1, row0:row0 + seg, n:n + lanes]
                acc = acc + wdw_ref[k:k + 1, n:n + lanes] * src
            cv_scr[s_i * seg:(s_i + 1) * seg, n:n + lanes] = acc
    cv = cv_scr[...]
    mu = jnp.mean(cv, axis=-1, keepdims=True)
    xc = cv - mu
    var = jnp.mean(xc * xc, axis=-1, keepdims=True)
    ln = xc * lax.rsqrt(var + EPS) * lng_ref[...] + lnb_ref[...]
    out = _dot(_silu(ln), w2_ref[...]) + b2_ref[...]
    x_new = x + mod[2:3] * out
    xo_ref[...] = x_new
    _moe_pre(x_new, mod, g2_ref[...], wr_ref[...], br_ref[...], h2_ref, topi_ref, gate_ref, rank_ref, cnt_ref,
             cnt_scr)


def _conformer(xs, mod_i, g1, w_pw1, b_pw1, w_dw, b_dw, ln_g, ln_b, w_pw2, b_pw2, g2, w_router, b_router,
               with_ctx):
    bsz, s, d = xs.shape
    t0 = 0 if with_ctx else 1
    nt = s // TILE - t0
    ctx_row = bsz
    n_exp = w_router.shape[1]
    pad_rows = (TILE // GRID_W) * (GRID_W + 2 * CONV_PAD)
    pre_in, pre_out = _moe_pre_specs(d, n_exp)
    row = lambda v: v.astype(F32).reshape(1, -1)
    return pl.pallas_call(
        functools.partial(_cv_kernel, t0),
        grid=(bsz, nt),
        in_specs=[
            pl.BlockSpec((None, TILE, d), lambda b, t: (b, t + t0, 0)),
            pl.BlockSpec((None, 6, d), lambda b, t: (jnp.where(t + t0 == 0, ctx_row, b), 0, 0)),
            _full_spec((1, d)),
            _full_spec((d, 2 * d)), _full_spec((1, 2 * d)),
            _full_spec(w_dw.shape), _full_spec((1, d)), _full_spec((1, d)), _full_spec((1, d)),
            _full_spec((d, d)), _full_spec((1, d)),
        ] + pre_in,
        out_specs=pre_out,
        out_shape=_moe_pre_shapes(bsz, nt * TILE, d, n_exp),
        scratch_shapes=[
            pltpu.VMEM((pad_rows, d), F32),
            pltpu.VMEM((SUBLANES - 1, pad_rows, d), F32),
            pltpu.VMEM((TILE, d), F32),
            _moe_pre_scratch(n_exp),
        ],
        compiler_params=_cparams(2, 56),
        name="conformer",
    )(xs, mod_i, row(g1), w_pw1.astype(MXU_DTYPE), row(b_pw1), w_dw.astype(F32), row(b_dw), row(ln_g), row(ln_b),
      w_pw2.astype(MXU_DTYPE), row(b_pw2), row(g2), w_router.T.astype(F32),
      b_router.astype(F32).reshape(n_exp, 1))


def _moe_kernel(be_ref, flag_ref, x_ref, wgu_ref, bgu_ref, wdn_ref, bdn_ref, y_ref, wgu_scr, wdn_scr, act_scr):
    i = pl.program_id(0)
    flags = flag_ref[i]
    d_ff = wdn_ref.shape[0]

    @pl.when((flags & 2) != 0)
    def _():
        for n in range(0, 2 * d_ff, N_CHUNK):
            wgu_scr[:, n:n + N_CHUNK] = wgu_ref[:, n:n + N_CHUNK].astype(wgu_scr.dtype)
        for n in range(0, wdn_ref.shape[1], N_CHUNK):
            wdn_scr[:, n:n + N_CHUNK] = wdn_ref[:, n:n + N_CHUNK].astype(wdn_scr.dtype)

    @pl.when((flags & 1) != 0)
    def _():
        x = x_ref[...]
        for n in range(0, d_ff, N_CHUNK):
            gate = _dot(x, wgu_scr[:, n:n + N_CHUNK]) + bgu_ref[:, n:n + N_CHUNK]
            up = _dot(x, wgu_scr[:, d_ff + n:d_ff + n + N_CHUNK]) + bgu_ref[:, d_ff + n:d_ff + n + N_CHUNK]
            gate = jnp.minimum(gate, SWIGLU_LIMIT)
            up = jnp.clip(up, -SWIGLU_LIMIT, SWIGLU_LIMIT)
            glu = gate * _sigmoid(gate * SWIGLU_ALPHA)
            act_scr[:, n:n + N_CHUNK] = ((up + 1.0) * glu).astype(act_scr.dtype)
        act = act_scr[...]
        for n in range(0, y_ref.shape[1], N_CHUNK):
            y_ref[:, n:n + N_CHUNK] = (_dot(act, wdn_scr[:, n:n + N_CHUNK])
                                       + bdn_ref[:, n:n + N_CHUNK]).astype(y_ref.dtype)

    @pl.when((flags & 1) == 0)
    def _():
        y_ref[...] = jnp.zeros_like(y_ref)


def _moe_experts(xg, block_expert, block_flags, layer, w_gu, b_gu, w_down, b_down):
    n_rows, d = xg.shape
    depth, n_exp, _, d_gu = w_gu.shape
    d_ff = w_down.shape[2]
    nb = n_rows // MOE_TM
    grid_spec = pltpu.PrefetchScalarGridSpec(
        num_scalar_prefetch=2,
        grid=(nb,),
        in_specs=[
            pl.BlockSpec((MOE_TM, d), lambda i, be, fl: (i, 0)),
            pl.BlockSpec((None, None, d, d_gu), lambda i, be, fl: (layer, be[i], 0, 0)),
            pl.BlockSpec((None, None, 1, d_gu), lambda i, be, fl: (layer, be[i], 0, 0)),
            pl.BlockSpec((None, None, d_ff, d), lambda i, be, fl: (layer, be[i], 0, 0)),
            pl.BlockSpec((None, None, 1, d), lambda i, be, fl: (layer, be[i], 0, 0)),
        ],
        out_specs=pl.BlockSpec((MOE_TM, d), lambda i, be, fl: (i, 0)),
        scratch_shapes=[
            pltpu.VMEM((d, d_gu), MXU_DTYPE),
            pltpu.VMEM((d_ff, d), MXU_DTYPE),
            pltpu.VMEM((MOE_TM, d_ff), MXU_DTYPE),
        ],
    )
    return pl.pallas_call(
        _moe_kernel,
        grid_spec=grid_spec,
        out_shape=jax.ShapeDtypeStruct((n_rows, d), ACT_DTYPE),
        compiler_params=_cparams(1, 56),
        name="moe_experts",
    )(block_expert, block_flags, xg, w_gu, b_gu.reshape(depth, n_exp, 1, d_gu), w_down,
      b_down.reshape(depth, n_exp, 1, d))


def _take(rows, idx):
    return rows.at[idx].get(mode="promise_in_bounds")


def _route(topi, rank, counts):
    n_exp = counts.shape[0]
    n_assign = topi.size
    experts = jnp.arange(n_exp, dtype=jnp.int32)
    padded = (counts + MOE_TM - 1) // MOE_TM * MOE_TM
    pad_end = jnp.cumsum(padded)
    pad_start = pad_end - padded
    unpad_start = jnp.cumsum(counts) - counts
    start_of = jnp.sum(jnp.where(topi[..., None] == experts, pad_start, 0), axis=-1)
    slot_of_assign = start_of + rank
    nb = -(-n_assign // MOE_TM) + n_exp
    block_start = jnp.arange(nb, dtype=jnp.int32) * MOE_TM
    valid = block_start < pad_end[-1]
    be = jnp.minimum(jnp.sum((block_start[:, None] >= pad_end[None, :]).astype(jnp.int32), axis=1), n_exp - 1)
    n_valid = pad_end[-1] // MOE_TM
    be = jnp.where(valid, be, be[jnp.maximum(n_valid - 1, 0)])
    first = jnp.logical_and(valid, block_start == pad_start[be])
    flags = valid.astype(jnp.int32) + 2 * first.astype(jnp.int32)
    by_slot = jnp.argsort(slot_of_assign.reshape(-1)).astype(jnp.int32)
    n_tok = topi.shape[1]
    slot = jnp.arange(nb * MOE_TM, dtype=jnp.int32)
    per_slot = lambda per_block: jnp.repeat(per_block, MOE_TM)
    r_slot = slot - per_slot(pad_start[be])
    occupied = jnp.logical_and(per_slot(valid), r_slot < per_slot(counts[be]))
    src = jnp.clip(per_slot(unpad_start[be]) + r_slot, 0, n_assign - 1)
    tok_of_slot = jnp.where(occupied, _take(by_slot, src) % n_tok, slot % n_tok)
    return tok_of_slot, slot_of_assign, be, flags


def _combine_kernel(final, x_ref, mod_ref, g_ref, p0_ref, p1_ref, p2_ref, p3_ref, fg_ref, o_ref):
    g = g_ref[...]
    f = g[:, 0:1] * p0_ref[...].astype(F32)
    for k, p_ref in ((1, p1_ref), (2, p2_ref), (3, p3_ref)):
        f = f + g[:, k:k + 1] * p_ref[...].astype(F32)
    x = x_ref[...] + mod_ref[5:6, :] * f
    if final:
        ms = jnp.mean(x * x, axis=-1, keepdims=True)
        x = x * lax.rsqrt(ms + EPS) * fg_ref[...]
    o_ref[...] = x


def _combine(x_mid, mod_i, gates_t, picked, final_g, with_ctx, final):
    bsz, s, d = x_mid.shape
    nt = s // TILE
    ctx_row = bsz
    mod_row = (lambda b, t: jnp.where(t == 0, ctx_row, b)) if with_ctx else (lambda b, t: b)
    tok = pl.BlockSpec((None, TILE, d), lambda b, t: (b, t, 0))
    flat = pl.BlockSpec((TILE, d), lambda b, t: (b * nt + t, 0))
    return pl.pallas_call(
        functools.partial(_combine_kernel, final),
        grid=(bsz, nt),
        in_specs=[
            tok,
            pl.BlockSpec((None, 6, d), lambda b, t: (mod_row(b, t), 0, 0)),
            pl.BlockSpec((None, TILE, TOP_K), lambda b, t: (b, t, 0)),
            flat, flat, flat, flat,
            _full_spec((1, d)),
        ],
        out_specs=tok,
        out_shape=jax.ShapeDtypeStruct((bsz, s, d), F32),
        compiler_params=_cparams(2, 32),
        name="moe_combine",
    )(x_mid, mod_i, gates_t, *picked, final_g.astype(F32).reshape(1, d))


def _moe(x_mid, mod_i, h2, topi, gates, rank, counts, layer, w_gu, b_gu, w_down, b_down, final_g, with_ctx, final):
    bsz, s, d = h2.shape
    flat = lambda v: v.transpose(1, 0, 2).reshape(TOP_K, bsz * s)
    tok_of_slot, slot_of_assign, be, flags = _route(flat(topi), flat(rank), counts[:, 0].astype(jnp.int32))
    xg = _take(h2.reshape(bsz * s, d), tok_of_slot)
    y = _moe_experts(xg, be, flags, layer, w_gu, b_gu, w_down, b_down)
    picked = [_take(y, slot_of_assign[k]) for k in range(TOP_K)]
    return _combine(x_mid, mod_i, gates.transpose(0, 2, 1), picked, final_g, with_ctx, final)


def kernel(x, c, ctx, c_ctx, w_ada, b_ada, norm1_g, norm2_g, ssd_w_in, ssd_w_conv, ssd_b_conv, ssd_dt_bias, ssd_a_log, ssd_d, ssd_norm_g, ssd_w_out, cv_w_pw1, cv_b_pw1, cv_w_dw, cv_b_dw, cv_ln_g, cv_ln_b, cv_w_pw2, cv_b_pw2, moe_w_router, moe_b_router, moe_w_gu, moe_b_gu, moe_w_down, moe_b_down, final_g):
    bsz, seq, d = x.shape
    n_ctx = ctx.shape[1]
    depth = w_ada.shape[0]
    assert n_ctx == TILE and seq % TILE == 0 and TILE % GRID_W == 0 and TILE % SSD_CHUNK == 0
    assert depth % N_MIXERS == 0

    mod_rows = -(-(bsz + 1) // 8) * 8
    c_all = jnp.zeros((mod_rows, d), F32).at[:bsz].set(c).at[bsz].set(c_ctx)
    mods = _ada_mods(c_all, w_ada, b_ada)
    xs = jnp.concatenate([ctx, x], axis=1)

    for i in range(depth):
        j = i // N_MIXERS
        use_ssd = (i % N_MIXERS) == 0
        need_ctx = i < depth - 1
        mod_i = mods[i]
        if use_ssd:
            z_g, xbc, acum, acum_t = _ssd_inproj(xs, mod_i, norm1_g[i], ssd_w_in[j], ssd_dt_bias[j], ssd_a_log[j])
            xs_g, bmt_g, cm_g = _ssd_conv(xbc, ssd_w_conv[j], ssd_b_conv[j])
            s = xs.shape[1]
            acum_d = acum.reshape(bsz, s, 2, N_SSD_HEADS).transpose(2, 0, 1, 3)
            acum_t_d = acum_t.reshape(bsz, 2, N_SSD_HEADS, s).transpose(1, 0, 2, 3)
            y = _ssd_scan(xs_g, bmt_g, cm_g, acum_d, acum_t_d, n_ctx // SSD_CHUNK)
            x_mid, h2, topi, gates, rank, counts = _ssd_out(
                xs, mod_i, y, xs_g, z_g, ssd_d[j], ssd_norm_g[j], ssd_w_out[j], norm2_g[i], moe_w_router[i],
                moe_b_router[i])
        else:
            x_mid, h2, topi, gates, rank, counts = _conformer(
                xs, mod_i, norm1_g[i], cv_w_pw1[j], cv_b_pw1[j], cv_w_dw[j], cv_b_dw[j], cv_ln_g[j], cv_ln_b[j],
                cv_w_pw2[j], cv_b_pw2[j], norm2_g[i], moe_w_router[i], moe_b_router[i], need_ctx)
        xs = _moe(x_mid, mod_i, h2, topi, gates, rank, counts, i, moe_w_gu, moe_b_gu, moe_w_down, moe_b_down,
                  final_g, need_ctx, i == depth - 1)
    return xs
```

```python
import functools

import jax
import jax.numpy as jnp
from jax import lax
from jax.experimental import pallas as pl
from jax.experimental.pallas import tpu as pltpu

F32 = jnp.float32
BF16 = jnp.bfloat16
MXU_DTYPE = BF16
ACT_DTYPE = BF16

EPS = 1e-6
LOG2_E = 1.4426950408889634
GRID_W = 64
N_MIXERS = 2
HEAD_DIM = 64
N_SSD_HEADS = 32
N_SSD_GROUPS = 8
HEADS_PER_GROUP = N_SSD_HEADS // N_SSD_GROUPS
D_STATE = 128
SSD_CHUNK = 128
GROUP_W = HEADS_PER_GROUP * HEAD_DIM
D_INNER = N_SSD_HEADS * HEAD_DIM
D_BC = N_SSD_GROUPS * D_STATE
D_XBC = D_INNER + 2 * D_BC
TOP_K = 4
SWIGLU_ALPHA = 1.702
SWIGLU_LIMIT = 7.0

TILE = 256
CONV_PAD = 16
SSD_HALO = 8
N_CHUNK = 512
MOE_TM = 512
CNT_LANES = 128
SUBLANES = 8
OUT_ROWS = 128
CV_SEGS = 2
VMEM_MB = 1024 * 1024


def _cparams(n_axes, vmem_mb):
    return pltpu.CompilerParams(dimension_semantics=("arbitrary",) * n_axes,
                                vmem_limit_bytes=vmem_mb * VMEM_MB)


def _full_spec(shape):
    zeros = (0,) * len(shape)
    return pl.BlockSpec(shape, lambda *_: zeros)


def _dot(a, b):
    return jnp.dot(a.astype(MXU_DTYPE), b.astype(MXU_DTYPE), preferred_element_type=F32)


def _dot_nt(a, b):
    return lax.dot_general(a.astype(MXU_DTYPE), b.astype(MXU_DTYPE), (((1,), (1,)), ((), ())),
                           preferred_element_type=F32)


def _dot_tn(a, b):
    return lax.dot_general(a.astype(MXU_DTYPE), b.astype(MXU_DTYPE), (((0,), (0,)), ((), ())),
                           preferred_element_type=F32)


def _split3(v):
    hi = v.astype(BF16)
    r1 = v - hi.astype(F32)
    mid = r1.astype(BF16)
    lo = (r1 - mid.astype(F32)).astype(BF16)
    return hi, mid, lo


def _dot01_left(m01, v):
    hi, mid, lo = _split3(v)
    m = m01.astype(BF16)
    d = lambda p: jnp.dot(m, p, preferred_element_type=F32)
    return d(hi) + d(mid) + d(lo)


def _dot01_right(v, m01):
    hi, mid, lo = _split3(v)
    m = m01.astype(BF16)
    d = lambda p: jnp.dot(p, m, preferred_element_type=F32)
    return d(hi) + d(mid) + d(lo)


def _dot_nt_precise(a, b):
    ah = a.astype(BF16)
    al = (a - ah.astype(F32)).astype(BF16)
    bh = b.astype(BF16)
    bl = (b - bh.astype(F32)).astype(BF16)
    d = lambda p, q: lax.dot_general(p, q, (((1,), (1,)), ((), ())), preferred_element_type=F32)
    return d(ah, bh) + d(ah, bl) + d(al, bh)


def _sigmoid(v):
    return 1.0 / (1.0 + jnp.exp(-v))


def _silu(v):
    return v * _sigmoid(v)


def _softplus(v):
    return jnp.maximum(v, 0.0) + jnp.log(1.0 + jnp.exp(-jnp.abs(v)))


def _rms_mod(x, g, shift, scale):
    ms = jnp.mean(x * x, axis=-1, keepdims=True)
    y = x * lax.rsqrt(ms + EPS) * g
    return y * (1.0 + scale) + shift


def _ada_kernel(c_ref, w_ref, b_ref, o_ref):
    o_ref[...] = _dot(_silu(c_ref[...]), w_ref[...]) + b_ref[...]


def _ada_mods(c_all, w_ada, b_ada):
    depth, d, d6 = w_ada.shape
    n_mod = d6 // d
    rows = c_all.shape[0]
    out = pl.pallas_call(
        _ada_kernel,
        grid=(depth, n_mod),
        in_specs=[
            pl.BlockSpec((rows, d), lambda i, n: (0, 0)),
            pl.BlockSpec((None, d, d), lambda i, n: (i, 0, n)),
            pl.BlockSpec((None, None, 1, d), lambda i, n: (i, n, 0, 0)),
        ],
        out_specs=pl.BlockSpec((None, None, rows, d), lambda i, n: (i, n, 0, 0)),
        out_shape=jax.ShapeDtypeStruct((depth, n_mod, rows, d), F32),
        compiler_params=_cparams(2, 32),
        name="ada_mods",
    )(c_all, w_ada, b_ada.reshape(depth, n_mod, 1, d))
    return out.transpose(0, 2, 1, 3)


def _moe_pre_init(cnt_scr):
    @pl.when(jnp.logical_and(pl.program_id(0) == 0, pl.program_id(1) == 0))
    def _():
        cnt_scr[...] = jnp.zeros_like(cnt_scr)


def _moe_pre(x_new, r0, mod, g2, wr_t, b_r, h2_ref, topi_ref, gate_ref, rank_ref, cnt_ref, cnt_scr):
    rows = x_new.shape[0]
    rs = slice(r0, r0 + rows)
    h2 = _rms_mod(x_new, g2, mod[3:4], mod[4:5])
    h2_ref[rs, :] = h2.astype(h2_ref.dtype)
    logits = _dot_nt_precise(wr_t, h2) + b_r
    n_exp = logits.shape[0]
    eidx = lax.broadcasted_iota(jnp.int32, logits.shape, 0)
    vals, idxs = [], []
    cur = logits
    for _ in range(TOP_K):
        m = jnp.max(cur, axis=0, keepdims=True)
        idx = jnp.min(jnp.where(cur == m, eidx, n_exp), axis=0, keepdims=True)
        vals.append(m)
        idxs.append(idx)
        cur = jnp.where(eidx == idx, -jnp.inf, cur)
    es = [jnp.exp(v - vals[0]) for v in vals]
    tot = es[0] + es[1] + es[2] + es[3]
    t_src = lax.broadcasted_iota(jnp.int32, (rows, rows), 0)
    t_dst = lax.broadcasted_iota(jnp.int32, (rows, rows), 1)
    earlier = jnp.where(t_src < t_dst, 1.0, 0.0).astype(BF16)
    base = cnt_scr[:, 0:1]
    onehots = [jnp.where(eidx == idxs[k], 1.0, 0.0) for k in range(TOP_K)]
    chosen = onehots[0] + onehots[1] + onehots[2] + onehots[3]
    before = base + jnp.dot(chosen.astype(BF16), earlier, preferred_element_type=F32)
    for k in range(TOP_K):
        topi_ref[k:k + 1, rs] = idxs[k]
        gate_ref[k:k + 1, rs] = es[k] / tot
        rank_ref[k:k + 1, rs] = jnp.sum(onehots[k] * before, axis=0, keepdims=True).astype(jnp.int32)
    totals = jnp.broadcast_to(base + jnp.sum(chosen, axis=1, keepdims=True), cnt_scr.shape)
    cnt_scr[...] = totals
    cnt_ref[...] = totals


def _moe_pre_specs(d, n_exp):
    in_specs = [_full_spec((1, d)), _full_spec((n_exp, d)), _full_spec((n_exp, 1))]
    out_specs = [
        pl.BlockSpec((None, TILE, d), lambda b, t: (b, t, 0)),
        pl.BlockSpec((None, TILE, d), lambda b, t: (b, t, 0)),
        pl.BlockSpec((None, TOP_K, TILE), lambda b, t: (b, 0, t)),
        pl.BlockSpec((None, TOP_K, TILE), lambda b, t: (b, 0, t)),
        pl.BlockSpec((None, TOP_K, TILE), lambda b, t: (b, 0, t)),
        _full_spec((n_exp, CNT_LANES)),
    ]
    return in_specs, out_specs


def _moe_pre_shapes(bsz, s_out, d, n_exp):
    return [
        jax.ShapeDtypeStruct((bsz, s_out, d), F32),
        jax.ShapeDtypeStruct((bsz, s_out, d), ACT_DTYPE),
        jax.ShapeDtypeStruct((bsz, TOP_K, s_out), jnp.int32),
        jax.ShapeDtypeStruct((bsz, TOP_K, s_out), F32),
        jax.ShapeDtypeStruct((bsz, TOP_K, s_out), jnp.int32),
        jax.ShapeDtypeStruct((n_exp, CNT_LANES), F32),
    ]


def _moe_pre_scratch(n_exp):
    return pltpu.VMEM((n_exp, CNT_LANES), F32)


def _inproj_kernel(x_ref, mod_ref, g_ref, wz_ref, wx_ref, wdt_ref, wdt_t_ref, dtb_ref, dtb_t_ref,
                   a_ref, a_t_ref, z_ref, xbc_ref, acum_ref, acum_t_ref):
    mod = mod_ref[...]
    h = _rms_mod(x_ref[...], g_ref[...], mod[0:1], mod[1:2]).astype(MXU_DTYPE)
    for g in range(N_SSD_GROUPS):
        z_ref[g] = _dot(h, wz_ref[:, g * GROUP_W:(g + 1) * GROUP_W]).astype(z_ref.dtype)
    for n in range(0, D_XBC, N_CHUNK):
        xbc_ref[:, n:n + N_CHUNK] = _dot(h, wx_ref[:, n:n + N_CHUNK]).astype(xbc_ref.dtype)
    dt = _softplus(_dot(h, wdt_ref[...]) + dtb_ref[...])
    dt_t = _softplus(_dot_nt(wdt_t_ref[...], h) + dtb_t_ref[...])
    dta = dt * a_ref[...]
    dta_t = dt_t * a_t_ref[...]
    ii = lax.broadcasted_iota(jnp.int32, (TILE, TILE), 0)
    jj = lax.broadcasted_iota(jnp.int32, (TILE, TILE), 1)
    same = (ii // SSD_CHUNK) == (jj // SSD_CHUNK)
    lower = jnp.where(same, jnp.where(jj <= ii, 1.0, 0.0), 0.0)
    upper = jnp.where(same, jnp.where(jj >= ii, 1.0, 0.0), 0.0)
    col = lax.broadcasted_iota(jnp.int32, dta.shape, 1)
    acum_ref[...] = jnp.where(col < N_SSD_HEADS, _dot01_left(lower, dta), _dot01_left(upper, dta))
    row = lax.broadcasted_iota(jnp.int32, dta_t.shape, 0)
    acum_t = jnp.where(row < N_SSD_HEADS, _dot01_right(dta_t, upper), _dot01_right(dta_t, lower))
    acum_t_ref[...] = acum_t - jnp.log(dt_t) * LOG2_E


def _ssd_inproj(xs, mod_i, g1, w_in, dt_bias, a_log):
    bsz, s, d = xs.shape
    nt = s // TILE
    ctx_row = bsz
    wz = w_in[:, :D_INNER].astype(MXU_DTYPE)
    wx = w_in[:, D_INNER:D_INNER + D_XBC].astype(MXU_DTYPE)
    wdt = w_in[:, D_INNER + D_XBC:].astype(MXU_DTYPE)
    n_dt = 2 * N_SSD_HEADS
    a = -jnp.exp(a_log.astype(F32)).reshape(1, n_dt) * LOG2_E
    dtb = dt_bias.astype(F32).reshape(1, n_dt)
    tok = lambda w: pl.BlockSpec((None, TILE, w), lambda b, t: (b, t, 0))
    return pl.pallas_call(
        _inproj_kernel,
        grid=(bsz, nt),
        in_specs=[
            tok(d),
            pl.BlockSpec((None, 6, d), lambda b, t: (jnp.where(t == 0, ctx_row, b), 0, 0)),
            _full_spec((1, d)),
            _full_spec((d, D_INNER)), _full_spec((d, D_XBC)), _full_spec((d, n_dt)), _full_spec((n_dt, d)),
            _full_spec((1, n_dt)), _full_spec((n_dt, 1)), _full_spec((1, n_dt)), _full_spec((n_dt, 1)),
        ],
        out_specs=[
            pl.BlockSpec((None, N_SSD_GROUPS, TILE, GROUP_W), lambda b, t: (b, 0, t, 0)),
            tok(D_XBC), tok(n_dt),
            pl.BlockSpec((None, n_dt, TILE), lambda b, t: (b, 0, t)),
        ],
        out_shape=[
            jax.ShapeDtypeStruct((bsz, N_SSD_GROUPS, s, GROUP_W), ACT_DTYPE),
            jax.ShapeDtypeStruct((bsz, s, D_XBC), ACT_DTYPE),
            jax.ShapeDtypeStruct((bsz, s, n_dt), F32),
            jax.ShapeDtypeStruct((bsz, n_dt, s), F32),
        ],
        compiler_params=_cparams(2, 56),
        name="ssd_inproj",
    )(xs, mod_i, g1.reshape(1, d), wz, wx, wdt, wdt.T, dtb, dtb.T, a, a.T)


def _ssd_conv_kernel(nt, main_ref, prev_ref, next_ref, w_ref, b_ref, xs_ref, bmt_ref, cm_ref, shifted_scr, bm_scr):
    t = pl.program_id(1)
    width = w_ref.shape[0]
    half = (width - 1) // 2
    assert half <= SSD_HALO
    prev_ok = jnp.where(t >= 2, 1.0, 0.0)
    next_ok = jnp.where(jnp.logical_and(t >= 1, t < nt - 1), 1.0, 0.0)
    taps = [k for k in range(width) if k != half]
    r_out = lax.broadcasted_iota(jnp.int32, (TILE, TILE), 0)
    r_in = lax.broadcasted_iota(jnp.int32, (TILE, TILE), 1)
    shifts = jnp.concatenate(
        [jnp.where(r_in - r_out == k - half, 1.0, 0.0).astype(main_ref.dtype) for k in taps], axis=0)
    halo_row = lax.broadcasted_iota(jnp.int32, (SSD_HALO, 1), 0)
    rows = 64
    for n in range(0, D_XBC, N_CHUNK):
        cols = slice(n, n + N_CHUNK)
        u = main_ref[:, cols]
        shifted_scr[...] = jnp.dot(shifts, u, preferred_element_type=F32)
        pv = prev_ref[:, cols].astype(F32) * prev_ok
        nx = next_ref[:, cols].astype(F32) * next_ok
        head = jnp.zeros((SSD_HALO, N_CHUNK), F32)
        tail = jnp.zeros((SSD_HALO, N_CHUNK), F32)
        for k in range(width):
            m = abs(k - half)
            if k < half:
                head = head + jnp.where(halo_row < m, w_ref[k:k + 1, cols] * pltpu.roll(pv, m, axis=0), 0.0)
            elif k > half:
                tail = tail + jnp.where(halo_row >= SSD_HALO - m,
                                        w_ref[k:k + 1, cols] * pltpu.roll(nx, SSD_HALO - m, axis=0), 0.0)
        for r0 in range(0, TILE, rows):
            acc = b_ref[:, cols] + w_ref[half:half + 1, cols] * u[r0:r0 + rows].astype(F32)
            for i, k in enumerate(taps):
                acc = acc + w_ref[k:k + 1, cols] * shifted_scr[i * TILE + r0:i * TILE + r0 + rows, :]
            if r0 == 0:
                acc = jnp.concatenate([acc[:SSD_HALO] + head, acc[SSD_HALO:]], axis=0)
            if r0 == TILE - rows:
                acc = jnp.concatenate([acc[:rows - SSD_HALO], acc[rows - SSD_HALO:] + tail], axis=0)
            out = _silu(acc)
            for c0 in range(n, n + N_CHUNK, D_STATE):
                piece = out[:, c0 - n:c0 - n + D_STATE]
                if c0 < D_INNER:
                    g, o = divmod(c0, GROUP_W)
                    xs_ref[g, r0:r0 + rows, o:o + D_STATE] = piece.astype(xs_ref.dtype)
                elif c0 < D_INNER + D_BC:
                    bm_scr[r0:r0 + rows, c0 - D_INNER:c0 - D_INNER + D_STATE] = piece
                else:
                    cm_ref[(c0 - D_INNER - D_BC) // D_STATE, r0:r0 + rows, :] = piece.astype(cm_ref.dtype)
    for g in range(N_SSD_GROUPS):
        for r0 in range(0, TILE, D_STATE):
            blk = bm_scr[r0:r0 + D_STATE, g * D_STATE:(g + 1) * D_STATE]
            bmt_ref[g, :, r0:r0 + D_STATE] = blk.T.astype(bmt_ref.dtype)


def _ssd_conv(xbc, w_conv, b_conv):
    bsz, s, c = xbc.shape
    nt = s // TILE
    per_tile = TILE // SSD_HALO
    last_halo = s // SSD_HALO - 1
    grp = lambda w: pl.BlockSpec((None, N_SSD_GROUPS, TILE, w), lambda b, t: (b, 0, t, 0))
    return pl.pallas_call(
        functools.partial(_ssd_conv_kernel, nt),
        grid=(bsz, nt),
        in_specs=[
            pl.BlockSpec((None, TILE, c), lambda b, t: (b, t, 0)),
            pl.BlockSpec((None, SSD_HALO, c), lambda b, t: (b, jnp.maximum(t * per_tile - 1, 0), 0)),
            pl.BlockSpec((None, SSD_HALO, c), lambda b, t: (b, jnp.minimum((t + 1) * per_tile, last_halo), 0)),
            _full_spec(w_conv.shape), _full_spec((1, c)),
        ],
        out_specs=[
            grp(GROUP_W),
            pl.BlockSpec((None, N_SSD_GROUPS, D_STATE, TILE), lambda b, t: (b, 0, 0, t)),
            grp(D_STATE),
        ],
        out_shape=[
            jax.ShapeDtypeStruct((bsz, N_SSD_GROUPS, s, GROUP_W), ACT_DTYPE),
            jax.ShapeDtypeStruct((bsz, N_SSD_GROUPS, D_STATE, s), ACT_DTYPE),
            jax.ShapeDtypeStruct((bsz, N_SSD_GROUPS, s, D_STATE), ACT_DTYPE),
        ],
        scratch_shapes=[
            pltpu.VMEM(((w_conv.shape[0] - 1) * TILE, N_CHUNK), F32),
            pltpu.VMEM((TILE, D_BC), F32),
        ],
        compiler_params=_cparams(2, 48),
        name="ssd_conv",
    )(xbc, xbc, xbc, w_conv.astype(F32), b_conv.astype(F32).reshape(1, c))


def _ssd_scan_kernel(xs_ref, bmt_ref, cm_ref, acol_ref, arow_ref, y_ref, st_ref):
    d = pl.program_id(0)
    c = pl.program_id(2)

    @pl.when(c == 0)
    def _():
        st_ref[...] = jnp.zeros_like(st_ref)

    q = SSD_CHUNK
    ii = lax.broadcasted_iota(jnp.int32, (q, q), 0)
    jj = lax.broadcasted_iota(jnp.int32, (q, q), 1)
    mask = (ii - jj) * (1 - 2 * d) >= 0
    pair_w = 2 * HEAD_DIM
    first_of_pair = lax.broadcasted_iota(jnp.int32, (1, pair_w), 1) < HEAD_DIM
    def scores_of(g):
        return _dot(cm_ref[g], bmt_ref[g])

    def factors_of(g, scores):
        hs = slice(g * HEADS_PER_GROUP, (g + 1) * HEADS_PER_GROUP)
        bmt_f = bmt_ref[g].astype(F32)
        cm_f = cm_ref[g].astype(F32)
        acol = acol_ref[:, hs]
        arow = arow_ref[hs, :]
        a_tot = jnp.where(d == 0, acol[q - 1:q, :], acol[0:1, :])
        lhs, new = [], []
        for r in range(HEADS_PER_GROUP):
            a_i = jnp.broadcast_to(acol[:, r:r + 1], (q, q))
            within = scores * jnp.where(mask, jnp.exp2(a_i - arow[r:r + 1, :]), 0.0)
            carried = cm_f * jnp.exp2(a_i)
            lhs.append(jnp.concatenate([within.astype(MXU_DTYPE), carried.astype(MXU_DTYPE)], axis=1))
            new.append((bmt_f * jnp.exp2(a_tot[:, r:r + 1] - arow[r:r + 1, :])).astype(MXU_DTYPE))
        return lhs, new, a_tot

    ahead = [scores_of(0)]
    for g in range(N_SSD_GROUPS):
        if g + 1 < N_SSD_GROUPS:
            ahead.append(scores_of(g + 1))
        lhs, new, a_tot = factors_of(g, ahead[g])
        x = xs_ref[g].astype(MXU_DTYPE)
        st = st_ref[g]
        st_m = st.astype(MXU_DTYPE)
        for p in range(HEADS_PER_GROUP // 2):
            lanes = slice(p * pair_w, (p + 1) * pair_w)
            x_pair = x[:, lanes]
            rhs = jnp.concatenate([x_pair, st_m[:, lanes]], axis=0)
            both = _dot(jnp.concatenate([lhs[2 * p], lhs[2 * p + 1]], axis=0), rhs)
            y_pair = jnp.where(first_of_pair, both[:q], both[q:])
            y_ref[g, :, lanes] = y_pair.astype(y_ref.dtype)
            keep = jnp.exp2(jnp.where(first_of_pair, a_tot[:, 2 * p:2 * p + 1], a_tot[:, 2 * p + 1:2 * p + 2]))
            zero = jnp.zeros_like(x_pair)
            x_split = jnp.concatenate([jnp.where(first_of_pair, x_pair, zero),
                                       jnp.where(first_of_pair, zero, x_pair)], axis=0)
            b_both = jnp.concatenate([new[2 * p], new[2 * p + 1]], axis=1)
            st_ref[g, :, lanes] = st[:, lanes] * keep + _dot(b_both, x_split)


def _ssd_scan(xs_g, bmt_g, cm_g, acum_d, acum_t_d, n_ctx_chunks):
    bsz, n_grp, s, _ = xs_g.shape
    nc = s // SSD_CHUNK
    last = nc - 1 + n_ctx_chunks

    def chunk(d, c):
        back = jnp.where(c < n_ctx_chunks, n_ctx_chunks - 1 - c, last - c)
        return jnp.where(d == 0, c, back)

    big = lambda w: pl.BlockSpec((None, n_grp, SSD_CHUNK, w), lambda d, b, c: (b, 0, chunk(d, c), 0))
    col = pl.BlockSpec((None, None, SSD_CHUNK, N_SSD_HEADS), lambda d, b, c: (d, b, chunk(d, c), 0))
    return pl.pallas_call(
        _ssd_scan_kernel,
        grid=(2, bsz, nc),
        in_specs=[
            big(GROUP_W),
            pl.BlockSpec((None, n_grp, D_STATE, SSD_CHUNK), lambda d, b, c: (b, 0, 0, chunk(d, c))),
            big(D_STATE), col,
            pl.BlockSpec((None, None, N_SSD_HEADS, SSD_CHUNK), lambda d, b, c: (d, b, 0, chunk(d, c))),
        ],
        out_specs=pl.BlockSpec((None, None, n_grp, SSD_CHUNK, GROUP_W),
                               lambda d, b, c: (d, b, 0, chunk(d, c), 0)),
        out_shape=jax.ShapeDtypeStruct((2, bsz, n_grp, s, GROUP_W), ACT_DTYPE),
        scratch_shapes=[pltpu.VMEM((n_grp, D_STATE, GROUP_W), F32)],
        compiler_params=_cparams(3, 32),
        name="ssd_scan",
    )(xs_g, bmt_g, cm_g, acum_d, acum_t_d)


def _ssd_out_kernel(x_ref, mod_ref, yf_ref, yb_ref, xs_ref, z_ref, dsk_ref, ng_ref, wo_ref,
                    g2_ref, wr_ref, br_ref, xo_ref, h2_ref, topi_ref, gate_ref, rank_ref, cnt_ref,
                    yn_scr, cnt_scr):
    mod = mod_ref[...]
    _moe_pre_init(cnt_scr)
    blocks = [slice(r0, r0 + OUT_ROWS) for r0 in range(0, TILE, OUT_ROWS)]
    for rs in blocks:
        for g in range(N_SSD_GROUPS):
            y = (yf_ref[g, rs, :].astype(F32) + yb_ref[g, rs, :].astype(F32)
                 + dsk_ref[g] * xs_ref[g, rs, :].astype(F32))
            y = y * _silu(z_ref[g, rs, :].astype(F32))
            ms = jnp.mean(y * y, axis=-1, keepdims=True)
            yn_scr[rs, g * GROUP_W:(g + 1) * GROUP_W] = (y * lax.rsqrt(ms + EPS) * ng_ref[g]).astype(yn_scr.dtype)
    x_new = [x_ref[rs, :] + mod[2:3] * _dot(yn_scr[rs, :], wo_ref[...]) for rs in blocks]
    for rs, xb in zip(blocks, x_new):
        xo_ref[rs, :] = xb
        _moe_pre(xb, rs.start, mod, g2_ref[...], wr_ref[...], br_ref[...], h2_ref, topi_ref, gate_ref, rank_ref,
                 cnt_ref, cnt_scr)


def _ssd_out(xs, mod_i, y, xs_g, z_g, d_skip, norm_g, w_out, g2, w_router, b_router):
    bsz, s, d = xs.shape
    nt = s // TILE
    ctx_row = bsz
    n_exp = w_router.shape[1]
    grp = lambda: pl.BlockSpec((None, N_SSD_GROUPS, TILE, GROUP_W), lambda b, t: (b, 0, t, 0))
    ydir = lambda dd: pl.BlockSpec((None, None, N_SSD_GROUPS, TILE, GROUP_W), lambda b, t: (dd, b, 0, t, 0))
    pre_in, pre_out = _moe_pre_specs(d, n_exp)
    dsk = jnp.repeat(d_skip.astype(F32), HEAD_DIM).reshape(N_SSD_GROUPS, 1, GROUP_W)
    return pl.pallas_call(
        _ssd_out_kernel,
        grid=(bsz, nt),
        in_specs=[
            pl.BlockSpec((None, TILE, d), lambda b, t: (b, t, 0)),
            pl.BlockSpec((None, 6, d), lambda b, t: (jnp.where(t == 0, ctx_row, b), 0, 0)),
            ydir(0), ydir(1), grp(), grp(),
            _full_spec((N_SSD_GROUPS, 1, GROUP_W)), _full_spec((N_SSD_GROUPS, 1, GROUP_W)),
            _full_spec((D_INNER, d)),
        ] + pre_in,
        out_specs=pre_out,
        out_shape=_moe_pre_shapes(bsz, s, d, n_exp),
        scratch_shapes=[pltpu.VMEM((TILE, D_INNER), MXU_DTYPE), _moe_pre_scratch(n_exp)],
        compiler_params=_cparams(2, 48),
        name="ssd_out",
    )(xs, mod_i, y, y, xs_g, z_g, dsk, norm_g.astype(F32).reshape(N_SSD_GROUPS, 1, GROUP_W),
      w_out.astype(MXU_DTYPE), g2.reshape(1, d), w_router.T.astype(F32), b_router.astype(F32).reshape(n_exp, 1))


def _cv_kernel(t0, x_ref, mod_ref, g1_ref, w1_ref, b1_ref, wdw_ref, bdw_ref, lng_ref, lnb_ref, w2_ref, b2_ref,
               g2_ref, wr_ref, br_ref, xo_ref, h2_ref, topi_ref, gate_ref, rank_ref, cnt_ref,
               pad_scr, sh_scr, cv_scr, cnt_scr):
    tile = pl.program_id(1) + t0
    d = x_ref.shape[-1]
    x = x_ref[...]
    mod = mod_ref[...]
    h = _rms_mod(x, g1_ref[...], mod[0:1], mod[1:2]).astype(MXU_DTYPE)
    width = wdw_ref.shape[0]
    half = (width - 1) // 2
    seg = GRID_W
    n_seg = TILE // seg
    stride = seg + 2 * CONV_PAD
    joined = jnp.where(tile == 0, 1.0, 0.0)
    zeros = jnp.zeros((CONV_PAD, d), F32)
    pad_scr[0:CONV_PAD, :] = zeros
    pad_scr[n_seg * stride - CONV_PAD:n_seg * stride, :] = zeros
    for n in range(0, d, N_CHUNK):
        a = _dot(h, w1_ref[:, n:n + N_CHUNK]) + b1_ref[:, n:n + N_CHUNK]
        gate = _dot(h, w1_ref[:, d + n:d + n + N_CHUNK]) + b1_ref[:, d + n:d + n + N_CHUNK]
        u = a * _sigmoid(gate)
        for s_i in range(n_seg):
            base = s_i * stride
            pad_scr[base + CONV_PAD:base + CONV_PAD + seg, n:n + N_CHUNK] = u[s_i * seg:(s_i + 1) * seg]
            if s_i > 0:
                pad_scr[base:base + CONV_PAD, n:n + N_CHUNK] = u[s_i * seg - CONV_PAD:s_i * seg] * joined
            if s_i < n_seg - 1:
                pad_scr[base + CONV_PAD + seg:base + stride, n:n + N_CHUNK] = (
                    u[(s_i + 1) * seg:(s_i + 1) * seg + CONV_PAD] * joined)
    lanes = 256
    pad_rows = n_seg * stride
    for n in range(0, d, lanes):
        blk = pad_scr[:, n:n + lanes]
        for s in range(1, SUBLANES):
            sh_scr[s - 1, :, n:n + lanes] = pltpu.roll(blk, pad_rows - s, axis=0)
    blocks = [range(s0, s0 + CV_SEGS) for s0 in range(0, n_seg, CV_SEGS)]
    x_new = []
    for segs in blocks:
        for s_i in segs:
            for n in range(0, d, lanes):
                acc = jnp.zeros((seg, lanes), F32) + bdw_ref[:, n:n + lanes]
                for k in range(width):
                    whole, phase = divmod(CONV_PAD + k - half, SUBLANES)
                    row0 = s_i * stride + whole * SUBLANES
                    if phase == 0:
                        src = pad_scr[row0:row0 + seg, n:n + lanes]
                    else:
                        src = sh_scr[phase - 1, row0:row0 + seg, n:n + lanes]
                    acc = acc + wdw_ref[k:k + 1, n:n + lanes] * src
                cv_scr[s_i * seg:(s_i + 1) * seg, n:n + lanes] = acc
        rs = slice(segs[0] * seg, (segs[-1] + 1) * seg)
        cv = cv_scr[rs, :]
        mu = jnp.mean(cv, axis=-1, keepdims=True)
        xc = cv - mu
        var = jnp.mean(xc * xc, axis=-1, keepdims=True)
        ln = xc * lax.rsqrt(var + EPS) * lng_ref[...] + lnb_ref[...]
        out = _dot(_silu(ln), w2_ref[...]) + b2_ref[...]
        x_new.append(x_ref[rs, :] + mod[2:3] * out)
    _moe_pre_init(cnt_scr)
    for segs, xb in zip(blocks, x_new):
        r0 = segs[0] * seg
        xo_ref[r0:r0 + xb.shape[0], :] = xb
        _moe_pre(xb, r0, mod, g2_ref[...], wr_ref[...], br_ref[...], h2_ref, topi_ref, gate_ref, rank_ref, cnt_ref,
                 cnt_scr)


def _conformer(xs, mod_i, g1, w_pw1, b_pw1, w_dw, b_dw, ln_g, ln_b, w_pw2, b_pw2, g2, w_router, b_router,
               with_ctx):
    bsz, s, d = xs.shape
    t0 = 0 if with_ctx else 1
    nt = s // TILE - t0
    ctx_row = bsz
    n_exp = w_router.shape[1]
    pad_rows = (TILE // GRID_W) * (GRID_W + 2 * CONV_PAD)
    pre_in, pre_out = _moe_pre_specs(d, n_exp)
    row = lambda v: v.astype(F32).reshape(1, -1)
    return pl.pallas_call(
        functools.partial(_cv_kernel, t0),
        grid=(bsz, nt),
        in_specs=[
            pl.BlockSpec((None, TILE, d), lambda b, t: (b, t + t0, 0)),
            pl.BlockSpec((None, 6, d), lambda b, t: (jnp.where(t + t0 == 0, ctx_row, b), 0, 0)),
            _full_spec((1, d)),
            _full_spec((d, 2 * d)), _full_spec((1, 2 * d)),
            _full_spec(w_dw.shape), _full_spec((1, d)), _full_spec((1, d)), _full_spec((1, d)),
            _full_spec((d, d)), _full_spec((1, d)),
        ] + pre_in,
        out_specs=pre_out,
        out_shape=_moe_pre_shapes(bsz, nt * TILE, d, n_exp),
        scratch_shapes=[
            pltpu.VMEM((pad_rows, d), F32),
            pltpu.VMEM((SUBLANES - 1, pad_rows, d), F32),
            pltpu.VMEM((TILE, d), F32),
            _moe_pre_scratch(n_exp),
        ],
        compiler_params=_cparams(2, 56),
        name="conformer",
    )(xs, mod_i, row(g1), w_pw1.astype(MXU_DTYPE), row(b_pw1), w_dw.astype(F32), row(b_dw), row(ln_g), row(ln_b),
      w_pw2.astype(MXU_DTYPE), row(b_pw2), row(g2), w_router.T.astype(F32),
      b_router.astype(F32).reshape(n_exp, 1))


def _moe_kernel(be_ref, flag_ref, x_ref, wgu_ref, bgu_ref, wdn_ref, bdn_ref, y_ref, wgu_scr, wdn_scr, act_scr):
    i = pl.program_id(0)
    flags = flag_ref[i]
    d_ff = wdn_ref.shape[0]

    @pl.when((flags & 2) != 0)
    def _():
        for n in range(0, 2 * d_ff, N_CHUNK):
            wgu_scr[:, n:n + N_CHUNK] = wgu_ref[:, n:n + N_CHUNK].astype(wgu_scr.dtype)
        for n in range(0, wdn_ref.shape[1], N_CHUNK):
            wdn_scr[:, n:n + N_CHUNK] = wdn_ref[:, n:n + N_CHUNK].astype(wdn_scr.dtype)

    @pl.when((flags & 1) != 0)
    def _():
        x = x_ref[...]
        for n in range(0, d_ff, N_CHUNK):
            gate = _dot(x, wgu_scr[:, n:n + N_CHUNK]) + bgu_ref[:, n:n + N_CHUNK]
            up = _dot(x, wgu_scr[:, d_ff + n:d_ff + n + N_CHUNK]) + bgu_ref[:, d_ff + n:d_ff + n + N_CHUNK]
            gate = jnp.minimum(gate, SWIGLU_LIMIT)
            up = jnp.clip(up, -SWIGLU_LIMIT, SWIGLU_LIMIT)
            glu = gate * _sigmoid(gate * SWIGLU_ALPHA)
            act_scr[:, n:n + N_CHUNK] = ((up + 1.0) * glu).astype(act_scr.dtype)
        act = act_scr[...]
        for n in range(0, y_ref.shape[1], N_CHUNK):
            y_ref[:, n:n + N_CHUNK] = (_dot(act, wdn_scr[:, n:n + N_CHUNK])
                                       + bdn_ref[:, n:n + N_CHUNK]).astype(y_ref.dtype)

    @pl.when((flags & 1) == 0)
    def _():
        y_ref[...] = jnp.zeros_like(y_ref)


def _moe_experts(xg, block_expert, block_flags, layer, w_gu, b_gu, w_down, b_down):
    n_rows, d = xg.shape
    depth, n_exp, _, d_gu = w_gu.shape
    d_ff = w_down.shape[2]
    nb = n_rows // MOE_TM
    grid_spec = pltpu.PrefetchScalarGridSpec(
        num_scalar_prefetch=2,
        grid=(nb,),
        in_specs=[
            pl.BlockSpec((MOE_TM, d), lambda i, be, fl: (i, 0)),
            pl.BlockSpec((None, None, d, d_gu), lambda i, be, fl: (layer, be[i], 0, 0)),
            pl.BlockSpec((None, None, 1, d_gu), lambda i, be, fl: (layer, be[i], 0, 0)),
            pl.BlockSpec((None, None, d_ff, d), lambda i, be, fl: (layer, be[i], 0, 0)),
            pl.BlockSpec((None, None, 1, d), lambda i, be, fl: (layer, be[i], 0, 0)),
        ],
        out_specs=pl.BlockSpec((MOE_TM, d), lambda i, be, fl: (i, 0)),
        scratch_shapes=[
            pltpu.VMEM((d, d_gu), MXU_DTYPE),
            pltpu.VMEM((d_ff, d), MXU_DTYPE),
            pltpu.VMEM((MOE_TM, d_ff), MXU_DTYPE),
        ],
    )
    return pl.pallas_call(
        _moe_kernel,
        grid_spec=grid_spec,
        out_shape=jax.ShapeDtypeStruct((n_rows, d), ACT_DTYPE),
        compiler_params=_cparams(1, 56),
        name="moe_experts",
    )(block_expert, block_flags, xg, w_gu, b_gu.reshape(depth, n_exp, 1, d_gu), w_down,
      b_down.reshape(depth, n_exp, 1, d))


def _take(rows, idx):
    return rows.at[idx].get(mode="promise_in_bounds")


def _route(topi, rank, counts):
    n_exp = counts.shape[0]
    n_assign = topi.size
    experts = jnp.arange(n_exp, dtype=jnp.int32)
    padded = (counts + MOE_TM - 1) // MOE_TM * MOE_TM
    pad_end = jnp.cumsum(padded)
    pad_start = pad_end - padded
    unpad_start = jnp.cumsum(counts) - counts
    start_of = jnp.sum(jnp.where(topi[..., None] == experts, pad_start, 0), axis=-1)
    slot_of_assign = start_of + rank
    nb = -(-n_assign // MOE_TM) + n_exp
    block_start = jnp.arange(nb, dtype=jnp.int32) * MOE_TM
    valid = block_start < pad_end[-1]
    be = jnp.minimum(jnp.sum((block_start[:, None] >= pad_end[None, :]).astype(jnp.int32), axis=1), n_exp - 1)
    n_valid = pad_end[-1] // MOE_TM
    be = jnp.where(valid, be, be[jnp.maximum(n_valid - 1, 0)])
    first = jnp.logical_and(valid, block_start == pad_start[be])
    flags = valid.astype(jnp.int32) + 2 * first.astype(jnp.int32)
    by_slot = jnp.argsort(slot_of_assign.reshape(-1)).astype(jnp.int32)
    n_tok = topi.shape[1]
    slot = jnp.arange(nb * MOE_TM, dtype=jnp.int32)
    per_slot = lambda per_block: jnp.repeat(per_block, MOE_TM)
    r_slot = slot - per_slot(pad_start[be])
    occupied = jnp.logical_and(per_slot(valid), r_slot < per_slot(counts[be]))
    src = jnp.clip(per_slot(unpad_start[be]) + r_slot, 0, n_assign - 1)
    tok_of_slot = jnp.where(occupied, _take(by_slot, src) % n_tok, slot % n_tok)
    return tok_of_slot, slot_of_assign, be, flags


def _combine_kernel(final, x_ref, mod_ref, g_ref, p0_ref, p1_ref, p2_ref, p3_ref, fg_ref, o_ref):
    g = g_ref[...]
    f = g[:, 0:1] * p0_ref[...].astype(F32)
    for k, p_ref in ((1, p1_ref), (2, p2_ref), (3, p3_ref)):
        f = f + g[:, k:k + 1] * p_ref[...].astype(F32)
    x = x_ref[...] + mod_ref[5:6, :] * f
    if final:
        ms = jnp.mean(x * x, axis=-1, keepdims=True)
        x = x * lax.rsqrt(ms + EPS) * fg_ref[...]
    o_ref[...] = x


def _combine(x_mid, mod_i, gates_t, picked, final_g, with_ctx, final):
    bsz, s, d = x_mid.shape
    nt = s // TILE
    ctx_row = bsz
    mod_row = (lambda b, t: jnp.where(t == 0, ctx_row, b)) if with_ctx else (lambda b, t: b)
    tok = pl.BlockSpec((None, TILE, d), lambda b, t: (b, t, 0))
    flat = pl.BlockSpec((TILE, d), lambda b, t: (b * nt + t, 0))
    return pl.pallas_call(
        functools.partial(_combine_kernel, final),
        grid=(bsz, nt),
        in_specs=[
            tok,
            pl.BlockSpec((None, 6, d), lambda b, t: (mod_row(b, t), 0, 0)),
            pl.BlockSpec((None, TILE, TOP_K), lambda b, t: (b, t, 0)),
            flat, flat, flat, flat,
            _full_spec((1, d)),
        ],
        out_specs=tok,
        out_shape=jax.ShapeDtypeStruct((bsz, s, d), F32),
        compiler_params=_cparams(2, 32),
        name="moe_combine",
    )(x_mid, mod_i, gates_t, *picked, final_g.astype(F32).reshape(1, d))


def _moe(x_mid, mod_i, h2, topi, gates, rank, counts, layer, w_gu, b_gu, w_down, b_down, final_g, with_ctx, final):
    bsz, s, d = h2.shape
    flat = lambda v: v.transpose(1, 0, 2).reshape(TOP_K, bsz * s)
    tok_of_slot, slot_of_assign, be, flags = _route(flat(topi), flat(rank), counts[:, 0].astype(jnp.int32))
    xg = _take(h2.reshape(bsz * s, d), tok_of_slot)
    y = _moe_experts(xg, be, flags, layer, w_gu, b_gu, w_down, b_down)
    picked = [_take(y, slot_of_assign[k]) for k in range(TOP_K)]
    return _combine(x_mid, mod_i, gates.transpose(0, 2, 1), picked, final_g, with_ctx, final)


def kernel(x, c, ctx, c_ctx, w_ada, b_ada, norm1_g, norm2_g, ssd_w_in, ssd_w_conv, ssd_b_conv, ssd_dt_bias, ssd_a_log, ssd_d, ssd_norm_g, ssd_w_out, cv_w_pw1, cv_b_pw1, cv_w_dw, cv_b_dw, cv_ln_g, cv_ln_b, cv_w_pw2, cv_b_pw2, moe_w_router, moe_b_router, moe_w_gu, moe_b_gu, moe_w_down, moe_b_down, final_g):
    bsz, seq, d = x.shape
    n_ctx = ctx.shape[1]
    depth = w_ada.shape[0]
    assert n_ctx == TILE and seq % TILE == 0 and TILE % GRID_W == 0 and TILE % SSD_CHUNK == 0
    assert depth % N_MIXERS == 0

    mod_rows = -(-(bsz + 1) // 8) * 8
    c_all = jnp.zeros((mod_rows, d), F32).at[:bsz].set(c).at[bsz].set(c_ctx)
    mods = _ada_mods(c_all, w_ada, b_ada)
    xs = jnp.concatenate([ctx, x], axis=1)

    for i in range(depth):
        j = i // N_MIXERS
        use_ssd = (i % N_MIXERS) == 0
        need_ctx = i < depth - 1
        mod_i = mods[i]
        if use_ssd:
            z_g, xbc, acum, acum_t = _ssd_inproj(xs, mod_i, norm1_g[i], ssd_w_in[j], ssd_dt_bias[j], ssd_a_log[j])
            xs_g, bmt_g, cm_g = _ssd_conv(xbc, ssd_w_conv[j], ssd_b_conv[j])
            s = xs.shape[1]
            acum_d = acum.reshape(bsz, s, 2, N_SSD_HEADS).transpose(2, 0, 1, 3)
            acum_t_d = acum_t.reshape(bsz, 2, N_SSD_HEADS, s).transpose(1, 0, 2, 3)
            y = _ssd_scan(xs_g, bmt_g, cm_g, acum_d, acum_t_d, n_ctx // SSD_CHUNK)
            x_mid, h2, topi, gates, rank, counts = _ssd_out(
                xs, mod_i, y, xs_g, z_g, ssd_d[j], ssd_norm_g[j], ssd_w_out[j], norm2_g[i], moe_w_router[i],
                moe_b_router[i])
        else:
            x_mid, h2, topi, gates, rank, counts = _conformer(
                xs, mod_i, norm1_g[i], cv_w_pw1[j], cv_b_pw1[j], cv_w_dw[j], cv_b_dw[j], cv_ln_g[j], cv_ln_b[j],
                cv_w_pw2[j], cv_b_pw2[j], norm2_g[i], moe_w_router[i], moe_b_router[i], need_ctx)
        xs = _moe(x_mid, mod_i, h2, topi, gates, rank, counts, i, moe_w_gu, moe_b_gu, moe_w_down, moe_b_down,
                  final_g, need_ctx, i == depth - 1)
    return xs
```

```python
import functools

import jax
import jax.numpy as jnp
from jax import lax
from jax.experimental import pallas as pl
from jax.experimental.pallas import tpu as pltpu

F32 = jnp.float32
BF16 = jnp.bfloat16
MXU_DTYPE = BF16
ACT_DTYPE = BF16

EPS = 1e-6
LOG2_E = 1.4426950408889634
GRID_W = 64
N_MIXERS = 2
HEAD_DIM = 64
N_SSD_HEADS = 32
N_SSD_GROUPS = 8
HEADS_PER_GROUP = N_SSD_HEADS // N_SSD_GROUPS
D_STATE = 128
SSD_CHUNK = 128
GROUP_W = HEADS_PER_GROUP * HEAD_DIM
D_INNER = N_SSD_HEADS * HEAD_DIM
D_BC = N_SSD_GROUPS * D_STATE
D_XBC = D_INNER + 2 * D_BC
TOP_K = 4
SWIGLU_ALPHA = 1.702
SWIGLU_LIMIT = 7.0

TILE = 256
CONV_PAD = 16
SSD_HALO = 8
N_CHUNK = 512
MOE_TM = 512
CNT_LANES = 128
SUBLANES = 8
OUT_ROWS = 128
CV_SEGS = 2
VMEM_MB = 1024 * 1024


def _cparams(n_axes, vmem_mb):
    return pltpu.CompilerParams(dimension_semantics=("arbitrary",) * n_axes,
                                vmem_limit_bytes=vmem_mb * VMEM_MB)


def _full_spec(shape):
    zeros = (0,) * len(shape)
    return pl.BlockSpec(shape, lambda *_: zeros)


def _dot(a, b):
    return jnp.dot(a.astype(MXU_DTYPE), b.astype(MXU_DTYPE), preferred_element_type=F32)


def _dot_nt(a, b):
    return lax.dot_general(a.astype(MXU_DTYPE), b.astype(MXU_DTYPE), (((1,), (1,)), ((), ())),
                           preferred_element_type=F32)


def _dot_tn(a, b):
    return lax.dot_general(a.astype(MXU_DTYPE), b.astype(MXU_DTYPE), (((0,), (0,)), ((), ())),
                           preferred_element_type=F32)


def _split3(v):
    hi = v.astype(BF16)
    r1 = v - hi.astype(F32)
    mid = r1.astype(BF16)
    lo = (r1 - mid.astype(F32)).astype(BF16)
    return hi, mid, lo


def _dot01_left(m01, v):
    hi, mid, lo = _split3(v)
    m = m01.astype(BF16)
    d = lambda p: jnp.dot(m, p, preferred_element_type=F32)
    return d(hi) + d(mid) + d(lo)


def _dot01_right(v, m01):
    hi, mid, lo = _split3(v)
    m = m01.astype(BF16)
    d = lambda p: jnp.dot(p, m, preferred_element_type=F32)
    return d(hi) + d(mid) + d(lo)


def _dot_nt_precise(a, b):
    ah = a.astype(BF16)
    al = (a - ah.astype(F32)).astype(BF16)
    bh = b.astype(BF16)
    bl = (b - bh.astype(F32)).astype(BF16)
    d = lambda p, q: lax.dot_general(p, q, (((1,), (1,)), ((), ())), preferred_element_type=F32)
    return d(ah, bh) + d(ah, bl) + d(al, bh)


def _sigmoid(v):
    return 1.0 / (1.0 + jnp.exp(-v))


def _silu(v):
    return v * _sigmoid(v)


def _softplus(v):
    return jnp.maximum(v, 0.0) + jnp.log(1.0 + jnp.exp(-jnp.abs(v)))


def _rms_mod(x, g, shift, scale):
    ms = jnp.mean(x * x, axis=-1, keepdims=True)
    y = x * lax.rsqrt(ms + EPS) * g
    return y * (1.0 + scale) + shift


def _ada_kernel(c_ref, w_ref, b_ref, o_ref):
    o_ref[...] = _dot(_silu(c_ref[...]), w_ref[...]) + b_ref[...]


def _ada_mods(c_all, w_ada, b_ada):
    depth, d, d6 = w_ada.shape
    n_mod = d6 // d
    rows = c_all.shape[0]
    out = pl.pallas_call(
        _ada_kernel,
        grid=(depth, n_mod),
        in_specs=[
            pl.BlockSpec((rows, d), lambda i, n: (0, 0)),
            pl.BlockSpec((None, d, d), lambda i, n: (i, 0, n)),
            pl.BlockSpec((None, None, 1, d), lambda i, n: (i, n, 0, 0)),
        ],
        out_specs=pl.BlockSpec((None, None, rows, d), lambda i, n: (i, n, 0, 0)),
        out_shape=jax.ShapeDtypeStruct((depth, n_mod, rows, d), F32),
        compiler_params=_cparams(2, 32),
        name="ada_mods",
    )(c_all, w_ada, b_ada.reshape(depth, n_mod, 1, d))
    return out.transpose(0, 2, 1, 3)


def _moe_pre_init(cnt_scr):
    @pl.when(jnp.logical_and(pl.program_id(0) == 0, pl.program_id(1) == 0))
    def _():
        cnt_scr[...] = jnp.zeros_like(cnt_scr)


def _moe_pre(x_new, r0, mod, g2, wr_t, b_r, h2_ref, topi_ref, gate_ref, rank_ref, cnt_ref, cnt_scr):
    rows = x_new.shape[0]
    rs = slice(r0, r0 + rows)
    h2 = _rms_mod(x_new, g2, mod[3:4], mod[4:5])
    h2_ref[rs, :] = h2.astype(h2_ref.dtype)
    logits = _dot_nt_precise(wr_t, h2) + b_r
    n_exp = logits.shape[0]
    eidx = lax.broadcasted_iota(jnp.int32, logits.shape, 0)
    vals, idxs = [], []
    cur = logits
    for _ in range(TOP_K):
        m = jnp.max(cur, axis=0, keepdims=True)
        idx = jnp.min(jnp.where(cur == m, eidx, n_exp), axis=0, keepdims=True)
        vals.append(m)
        idxs.append(idx)
        cur = jnp.where(eidx == idx, -jnp.inf, cur)
    es = [jnp.exp(v - vals[0]) for v in vals]
    tot = es[0] + es[1] + es[2] + es[3]
    t_src = lax.broadcasted_iota(jnp.int32, (rows, rows), 0)
    t_dst = lax.broadcasted_iota(jnp.int32, (rows, rows), 1)
    earlier = jnp.where(t_src < t_dst, 1.0, 0.0).astype(BF16)
    base = cnt_scr[:, 0:1]
    onehots = [jnp.where(eidx == idxs[k], 1.0, 0.0) for k in range(TOP_K)]
    chosen = onehots[0] + onehots[1] + onehots[2] + onehots[3]
    before = base + jnp.dot(chosen.astype(BF16), earlier, preferred_element_type=F32)
    for k in range(TOP_K):
        topi_ref[k:k + 1, rs] = idxs[k]
        gate_ref[k:k + 1, rs] = es[k] / tot
        rank_ref[k:k + 1, rs] = jnp.sum(onehots[k] * before, axis=0, keepdims=True).astype(jnp.int32)
    totals = jnp.broadcast_to(base + jnp.sum(chosen, axis=1, keepdims=True), cnt_scr.shape)
    cnt_scr[...] = totals
    cnt_ref[...] = totals


def _moe_pre_specs(d, n_exp):
    in_specs = [_full_spec((1, d)), _full_spec((n_exp, d)), _full_spec((n_exp, 1))]
    out_specs = [
        pl.BlockSpec((None, TILE, d), lambda b, t: (b, t, 0)),
        pl.BlockSpec((None, TILE, d), lambda b, t: (b, t, 0)),
        pl.BlockSpec((None, TOP_K, TILE), lambda b, t: (b, 0, t)),
        pl.BlockSpec((None, TOP_K, TILE), lambda b, t: (b, 0, t)),
        pl.BlockSpec((None, TOP_K, TILE), lambda b, t: (b, 0, t)),
        _full_spec((n_exp, CNT_LANES)),
    ]
    return in_specs, out_specs


def _moe_pre_shapes(bsz, s_out, d, n_exp):
    return [
        jax.ShapeDtypeStruct((bsz, s_out, d), F32),
        jax.ShapeDtypeStruct((bsz, s_out, d), ACT_DTYPE),
        jax.ShapeDtypeStruct((bsz, TOP_K, s_out), jnp.int32),
        jax.ShapeDtypeStruct((bsz, TOP_K, s_out), F32),
        jax.ShapeDtypeStruct((bsz, TOP_K, s_out), jnp.int32),
        jax.ShapeDtypeStruct((n_exp, CNT_LANES), F32),
    ]


def _moe_pre_scratch(n_exp):
    return pltpu.VMEM((n_exp, CNT_LANES), F32)


def _inproj_kernel(x_ref, mod_ref, g_ref, wz_ref, wx_ref, wdt_ref, wdt_t_ref, dtb_ref, dtb_t_ref,
                   a_ref, a_t_ref, z_ref, xbc_ref, acum_ref, acum_t_ref):
    mod = mod_ref[...]
    h = _rms_mod(x_ref[...], g_ref[...], mod[0:1], mod[1:2]).astype(MXU_DTYPE)
    for g in range(N_SSD_GROUPS):
        z_ref[g] = _dot(h, wz_ref[:, g * GROUP_W:(g + 1) * GROUP_W]).astype(z_ref.dtype)
    for n in range(0, D_XBC, N_CHUNK):
        xbc_ref[:, n:n + N_CHUNK] = _dot(h, wx_ref[:, n:n + N_CHUNK]).astype(xbc_ref.dtype)
    dt = _softplus(_dot(h, wdt_ref[...]) + dtb_ref[...])
    dt_t = _softplus(_dot_nt(wdt_t_ref[...], h) + dtb_t_ref[...])
    dta = dt * a_ref[...]
    dta_t = dt_t * a_t_ref[...]
    ii = lax.broadcasted_iota(jnp.int32, (TILE, TILE), 0)
    jj = lax.broadcasted_iota(jnp.int32, (TILE, TILE), 1)
    same = (ii // SSD_CHUNK) == (jj // SSD_CHUNK)
    lower = jnp.where(same, jnp.where(jj <= ii, 1.0, 0.0), 0.0)
    upper = jnp.where(same, jnp.where(jj >= ii, 1.0, 0.0), 0.0)
    col = lax.broadcasted_iota(jnp.int32, dta.shape, 1)
    acum_ref[...] = jnp.where(col < N_SSD_HEADS, _dot01_left(lower, dta), _dot01_left(upper, dta))
    row = lax.broadcasted_iota(jnp.int32, dta_t.shape, 0)
    acum_t = jnp.where(row < N_SSD_HEADS, _dot01_right(dta_t, upper), _dot01_right(dta_t, lower))
    acum_t_ref[...] = acum_t - jnp.log(dt_t) * LOG2_E


def _ssd_inproj(xs, mod_i, g1, w_in, dt_bias, a_log):
    bsz, s, d = xs.shape
    nt = s // TILE
    ctx_row = bsz
    wz = w_in[:, :D_INNER].astype(MXU_DTYPE)
    wx = w_in[:, D_INNER:D_INNER + D_XBC].astype(MXU_DTYPE)
    wdt = w_in[:, D_INNER + D_XBC:].astype(MXU_DTYPE)
    n_dt = 2 * N_SSD_HEADS
    a = -jnp.exp(a_log.astype(F32)).reshape(1, n_dt) * LOG2_E
    dtb = dt_bias.astype(F32).reshape(1, n_dt)
    tok = lambda w: pl.BlockSpec((None, TILE, w), lambda b, t: (b, t, 0))
    return pl.pallas_call(
        _inproj_kernel,
        grid=(bsz, nt),
        in_specs=[
            tok(d),
            pl.BlockSpec((None, 6, d), lambda b, t: (jnp.where(t == 0, ctx_row, b), 0, 0)),
            _full_spec((1, d)),
            _full_spec((d, D_INNER)), _full_spec((d, D_XBC)), _full_spec((d, n_dt)), _full_spec((n_dt, d)),
            _full_spec((1, n_dt)), _full_spec((n_dt, 1)), _full_spec((1, n_dt)), _full_spec((n_dt, 1)),
        ],
        out_specs=[
            pl.BlockSpec((None, N_SSD_GROUPS, TILE, GROUP_W), lambda b, t: (b, 0, t, 0)),
            tok(D_XBC), tok(n_dt),
            pl.BlockSpec((None, n_dt, TILE), lambda b, t: (b, 0, t)),
        ],
        out_shape=[
            jax.ShapeDtypeStruct((bsz, N_SSD_GROUPS, s, GROUP_W), ACT_DTYPE),
            jax.ShapeDtypeStruct((bsz, s, D_XBC), ACT_DTYPE),
            jax.ShapeDtypeStruct((bsz, s, n_dt), F32),
            jax.ShapeDtypeStruct((bsz, n_dt, s), F32),
        ],
        compiler_params=_cparams(2, 56),
        name="ssd_inproj",
    )(xs, mod_i, g1.reshape(1, d), wz, wx, wdt, wdt.T, dtb, dtb.T, a, a.T)


def _ssd_conv_kernel(nt, main_ref, prev_ref, next_ref, w_ref, b_ref, xs_ref, bmt_ref, cm_ref, shifted_scr, bm_scr):
    t = pl.program_id(1)
    width = w_ref.shape[0]
    half = (width - 1) // 2
    assert half <= SSD_HALO
    prev_ok = jnp.where(t >= 2, 1.0, 0.0)
    next_ok = jnp.where(jnp.logical_and(t >= 1, t < nt - 1), 1.0, 0.0)
    taps = [k for k in range(width) if k != half]
    r_out = lax.broadcasted_iota(jnp.int32, (TILE, TILE), 0)
    r_in = lax.broadcasted_iota(jnp.int32, (TILE, TILE), 1)
    shifts = jnp.concatenate(
        [jnp.where(r_in - r_out == k - half, 1.0, 0.0).astype(main_ref.dtype) for k in taps], axis=0)
    halo_row = lax.broadcasted_iota(jnp.int32, (SSD_HALO, 1), 0)
    rows = 64
    for n in range(0, D_XBC, N_CHUNK):
        cols = slice(n, n + N_CHUNK)
        u = main_ref[:, cols]
        shifted_scr[...] = jnp.dot(shifts, u, preferred_element_type=F32)
        pv = prev_ref[:, cols].astype(F32) * prev_ok
        nx = next_ref[:, cols].astype(F32) * next_ok
        head = jnp.zeros((SSD_HALO, N_CHUNK), F32)
        tail = jnp.zeros((SSD_HALO, N_CHUNK), F32)
        for k in range(width):
            m = abs(k - half)
            if k < half:
                head = head + jnp.where(halo_row < m, w_ref[k:k + 1, cols] * pltpu.roll(pv, m, axis=0), 0.0)
            elif k > half:
                tail = tail + jnp.where(halo_row >= SSD_HALO - m,
                                        w_ref[k:k + 1, cols] * pltpu.roll(nx, SSD_HALO - m, axis=0), 0.0)
        for r0 in range(0, TILE, rows):
            acc = b_ref[:, cols] + w_ref[half:half + 1, cols] * u[r0:r0 + rows].astype(F32)
            for i, k in enumerate(taps):
                acc = acc + w_ref[k:k + 1, cols] * shifted_scr[i * TILE + r0:i * TILE + r0 + rows, :]
            if r0 == 0:
                acc = jnp.concatenate([acc[:SSD_HALO] + head, acc[SSD_HALO:]], axis=0)
            if r0 == TILE - rows:
                acc = jnp.concatenate([acc[:rows - SSD_HALO], acc[rows - SSD_HALO:] + tail], axis=0)
            out = _silu(acc)
            for c0 in range(n, n + N_CHUNK, D_STATE):
                piece = out[:, c0 - n:c0 - n + D_STATE]
                if c0 < D_INNER:
                    g, o = divmod(c0, GROUP_W)
                    xs_ref[g, r0:r0 + rows, o:o + D_STATE] = piece.astype(xs_ref.dtype)
                elif c0 < D_INNER + D_BC:
                    bm_scr[r0:r0 + rows, c0 - D_INNER:c0 - D_INNER + D_STATE] = piece
                else:
                    cm_ref[(c0 - D_INNER - D_BC) // D_STATE, r0:r0 + rows, :] = piece.astype(cm_ref.dtype)
    for g in range(N_SSD_GROUPS):
        for r0 in range(0, TILE, D_STATE):
            blk = bm_scr[r0:r0 + D_STATE, g * D_STATE:(g + 1) * D_STATE]
            bmt_ref[g, :, r0:r0 + D_STATE] = blk.T.astype(bmt_ref.dtype)


def _ssd_conv(xbc, w_conv, b_conv):
    bsz, s, c = xbc.shape
    nt = s // TILE
    per_tile = TILE // SSD_HALO
    last_halo = s // SSD_HALO - 1
    grp = lambda w: pl.BlockSpec((None, N_SSD_GROUPS, TILE, w), lambda b, t: (b, 0, t, 0))
    return pl.pallas_call(
        functools.partial(_ssd_conv_kernel, nt),
        grid=(bsz, nt),
        in_specs=[
            pl.BlockSpec((None, TILE, c), lambda b, t: (b, t, 0)),
            pl.BlockSpec((None, SSD_HALO, c), lambda b, t: (b, jnp.maximum(t * per_tile - 1, 0), 0)),
            pl.BlockSpec((None, SSD_HALO, c), lambda b, t: (b, jnp.minimum((t + 1) * per_tile, last_halo), 0)),
            _full_spec(w_conv.shape), _full_spec((1, c)),
        ],
        out_specs=[
            grp(GROUP_W),
            pl.BlockSpec((None, N_SSD_GROUPS, D_STATE, TILE), lambda b, t: (b, 0, 0, t)),
            grp(D_STATE),
        ],
        out_shape=[
            jax.ShapeDtypeStruct((bsz, N_SSD_GROUPS, s, GROUP_W), ACT_DTYPE),
            jax.ShapeDtypeStruct((bsz, N_SSD_GROUPS, D_STATE, s), ACT_DTYPE),
            jax.ShapeDtypeStruct((bsz, N_SSD_GROUPS, s, D_STATE), ACT_DTYPE),
        ],
        scratch_shapes=[
            pltpu.VMEM(((w_conv.shape[0] - 1) * TILE, N_CHUNK), F32),
            pltpu.VMEM((TILE, D_BC), F32),
        ],
        compiler_params=_cparams(2, 48),
        name="ssd_conv",
    )(xbc, xbc, xbc, w_conv.astype(F32), b_conv.astype(F32).reshape(1, c))


def _ssd_scan_kernel(xs_ref, bmt_ref, cm_ref, acol_ref, arow_ref, y_ref, st_ref):
    d = pl.program_id(0)
    c = pl.program_id(2)

    @pl.when(c == 0)
    def _():
        st_ref[...] = jnp.zeros_like(st_ref)

    q = SSD_CHUNK
    ii = lax.broadcasted_iota(jnp.int32, (q, q), 0)
    jj = lax.broadcasted_iota(jnp.int32, (q, q), 1)
    mask = (ii - jj) * (1 - 2 * d) >= 0
    pair_w = 2 * HEAD_DIM
    first_of_pair = lax.broadcasted_iota(jnp.int32, (1, pair_w), 1) < HEAD_DIM
    def scores_of(g):
        return _dot(cm_ref[g], bmt_ref[g])

    def factors_of(g, scores):
        hs = slice(g * HEADS_PER_GROUP, (g + 1) * HEADS_PER_GROUP)
        bmt_f = bmt_ref[g].astype(F32)
        cm_f = cm_ref[g].astype(F32)
        acol = acol_ref[:, hs]
        arow = arow_ref[hs, :]
        a_tot = jnp.where(d == 0, acol[q - 1:q, :], acol[0:1, :])
        lhs, new = [], []
        for r in range(HEADS_PER_GROUP):
            a_i = jnp.broadcast_to(acol[:, r:r + 1], (q, q))
            within = scores * jnp.where(mask, jnp.exp2(a_i - arow[r:r + 1, :]), 0.0)
            carried = cm_f * jnp.exp2(a_i)
            lhs.append(jnp.concatenate([within.astype(MXU_DTYPE), carried.astype(MXU_DTYPE)], axis=1))
            new.append((bmt_f * jnp.exp2(a_tot[:, r:r + 1] - arow[r:r + 1, :])).astype(MXU_DTYPE))
        return lhs, new, a_tot

    ahead = [scores_of(0)]
    for g in range(N_SSD_GROUPS):
        if g + 1 < N_SSD_GROUPS:
            ahead.append(scores_of(g + 1))
        lhs, new, a_tot = factors_of(g, ahead[g])
        x = xs_ref[g].astype(MXU_DTYPE)
        st = st_ref[g]
        st_m = st.astype(MXU_DTYPE)
        for p in range(HEADS_PER_GROUP // 2):
            lanes = slice(p * pair_w, (p + 1) * pair_w)
            x_pair = x[:, lanes]
            rhs = jnp.concatenate([x_pair, st_m[:, lanes]], axis=0)
            both = _dot(jnp.concatenate([lhs[2 * p], lhs[2 * p + 1]], axis=0), rhs)
            y_pair = jnp.where(first_of_pair, both[:q], both[q:])
            y_ref[g, :, lanes] = y_pair.astype(y_ref.dtype)
            keep = jnp.exp2(jnp.where(first_of_pair, a_tot[:, 2 * p:2 * p + 1], a_tot[:, 2 * p + 1:2 * p + 2]))
            zero = jnp.zeros_like(x_pair)
            x_split = jnp.concatenate([jnp.where(first_of_pair, x_pair, zero),
                                       jnp.where(first_of_pair, zero, x_pair)], axis=0)
            b_both = jnp.concatenate([new[2 * p], new[2 * p + 1]], axis=1)
            st_ref[g, :, lanes] = st[:, lanes] * keep + _dot(b_both, x_split)


def _ssd_scan(xs_g, bmt_g, cm_g, acum_d, acum_t_d, n_ctx_chunks):
    bsz, n_grp, s, _ = xs_g.shape
    nc = s // SSD_CHUNK
    last = nc - 1 + n_ctx_chunks

    def chunk(d, c):
        back = jnp.where(c < n_ctx_chunks, n_ctx_chunks - 1 - c, last - c)
        return jnp.where(d == 0, c, back)

    big = lambda w: pl.BlockSpec((None, n_grp, SSD_CHUNK, w), lambda d, b, c: (b, 0, chunk(d, c), 0))
    col = pl.BlockSpec((None, None, SSD_CHUNK, N_SSD_HEADS), lambda d, b, c: (d, b, chunk(d, c), 0))
    return pl.pallas_call(
        _ssd_scan_kernel,
        grid=(2, bsz, nc),
        in_specs=[
            big(GROUP_W),
            pl.BlockSpec((None, n_grp, D_STATE, SSD_CHUNK), lambda d, b, c: (b, 0, 0, chunk(d, c))),
            big(D_STATE), col,
            pl.BlockSpec((None, None, N_SSD_HEADS, SSD_CHUNK), lambda d, b, c: (d, b, 0, chunk(d, c))),
        ],
        out_specs=pl.BlockSpec((None, None, n_grp, SSD_CHUNK, GROUP_W),
                               lambda d, b, c: (d, b, 0, chunk(d, c), 0)),
        out_shape=jax.ShapeDtypeStruct((2, bsz, n_grp, s, GROUP_W), ACT_DTYPE),
        scratch_shapes=[pltpu.VMEM((n_grp, D_STATE, GROUP_W), F32)],
        compiler_params=_cparams(3, 32),
        name="ssd_scan",
    )(xs_g, bmt_g, cm_g, acum_d, acum_t_d)


def _ssd_out_kernel(x_ref, mod_ref, yf_ref, yb_ref, xs_ref, z_ref, dsk_ref, ng_ref, wo_ref,
                    g2_ref, wr_ref, br_ref, xo_ref, h2_ref, topi_ref, gate_ref, rank_ref, cnt_ref,
                    yn_scr, cnt_scr):
    mod = mod_ref[...]
    _moe_pre_init(cnt_scr)
    blocks = [slice(r0, r0 + OUT_ROWS) for r0 in range(0, TILE, OUT_ROWS)]
    for rs in blocks:
        for g in range(N_SSD_GROUPS):
            y = (yf_ref[g, rs, :].astype(F32) + yb_ref[g, rs, :].astype(F32)
                 + dsk_ref[g] * xs_ref[g, rs, :].astype(F32))
            y = y * _silu(z_ref[g, rs, :].astype(F32))
            ms = jnp.mean(y * y, axis=-1, keepdims=True)
            yn_scr[rs, g * GROUP_W:(g + 1) * GROUP_W] = (y * lax.rsqrt(ms + EPS) * ng_ref[g]).astype(yn_scr.dtype)
    x_new = [x_ref[rs, :] + mod[2:3] * _dot(yn_scr[rs, :], wo_ref[...]) for rs in blocks]
    for rs, xb in zip(blocks, x_new):
        xo_ref[rs, :] = xb
        _moe_pre(xb, rs.start, mod, g2_ref[...], wr_ref[...], br_ref[...], h2_ref, topi_ref, gate_ref, rank_ref,
                 cnt_ref, cnt_scr)


def _ssd_out(xs, mod_i, y, xs_g, z_g, d_skip, norm_g, w_out, g2, w_router, b_router):
    bsz, s, d = xs.shape
    nt = s // TILE
    ctx_row = bsz
    n_exp = w_router.shape[1]
    grp = lambda: pl.BlockSpec((None, N_SSD_GROUPS, TILE, GROUP_W), lambda b, t: (b, 0, t, 0))
    ydir = lambda dd: pl.BlockSpec((None, None, N_SSD_GROUPS, TILE, GROUP_W), lambda b, t: (dd, b, 0, t, 0))
    pre_in, pre_out = _moe_pre_specs(d, n_exp)
    dsk = jnp.repeat(d_skip.astype(F32), HEAD_DIM).reshape(N_SSD_GROUPS, 1, GROUP_W)
    return pl.pallas_call(
        _ssd_out_kernel,
        grid=(bsz, nt),
        in_specs=[
            pl.BlockSpec((None, TILE, d), lambda b, t: (b, t, 0)),
            pl.BlockSpec((None, 6, d), lambda b, t: (jnp.where(t == 0, ctx_row, b), 0, 0)),
            ydir(0), ydir(1), grp(), grp(),
            _full_spec((N_SSD_GROUPS, 1, GROUP_W)), _full_spec((N_SSD_GROUPS, 1, GROUP_W)),
            _full_spec((D_INNER, d)),
        ] + pre_in,
        out_specs=pre_out,
        out_shape=_moe_pre_shapes(bsz, s, d, n_exp),
        scratch_shapes=[pltpu.VMEM((TILE, D_INNER), MXU_DTYPE), _moe_pre_scratch(n_exp)],
        compiler_params=_cparams(2, 48),
        name="ssd_out",
    )(xs, mod_i, y, y, xs_g, z_g, dsk, norm_g.astype(F32).reshape(N_SSD_GROUPS, 1, GROUP_W),
      w_out.astype(MXU_DTYPE), g2.reshape(1, d), w_router.T.astype(F32), b_router.astype(F32).reshape(n_exp, 1))


def _moe_combined(x_ref, mod_ref, g_ref, p_refs):
    g = g_ref[...]
    f = g[:, 0:1] * p_refs[0][...].astype(F32)
    for k in range(1, TOP_K):
        f = f + g[:, k:k + 1] * p_refs[k][...].astype(F32)
    return x_ref[...] + mod_ref[5:6, :] * f


def _cv_kernel(t0, xm_ref, pmod_ref, pg_ref, p0_ref, p1_ref, p2_ref, p3_ref,
               mod_ref, g1_ref, w1_ref, b1_ref, wdw_ref, bdw_ref, lng_ref, lnb_ref, w2_ref, b2_ref,
               g2_ref, wr_ref, br_ref, xo_ref, h2_ref, topi_ref, gate_ref, rank_ref, cnt_ref,
               pad_scr, sh_scr, cv_scr, cnt_scr, x_scr):
    tile = pl.program_id(1) + t0
    d = xm_ref.shape[-1]
    x = _moe_combined(xm_ref, pmod_ref, pg_ref, (p0_ref, p1_ref, p2_ref, p3_ref))
    x_scr[...] = x
    mod = mod_ref[...]
    h = _rms_mod(x, g1_ref[...], mod[0:1], mod[1:2]).astype(MXU_DTYPE)
    width = wdw_ref.shape[0]
    half = (width - 1) // 2
    seg = GRID_W
    n_seg = TILE // seg
    stride = seg + 2 * CONV_PAD
    joined = jnp.where(tile == 0, 1.0, 0.0)
    zeros = jnp.zeros((CONV_PAD, d), F32)
    pad_scr[0:CONV_PAD, :] = zeros
    pad_scr[n_seg * stride - CONV_PAD:n_seg * stride, :] = zeros
    for n in range(0, d, N_CHUNK):
        a = _dot(h, w1_ref[:, n:n + N_CHUNK]) + b1_ref[:, n:n + N_CHUNK]
        gate = _dot(h, w1_ref[:, d + n:d + n + N_CHUNK]) + b1_ref[:, d + n:d + n + N_CHUNK]
        u = a * _sigmoid(gate)
        for s_i in range(n_seg):
            base = s_i * stride
            pad_scr[base + CONV_PAD:base + CONV_PAD + seg, n:n + N_CHUNK] = u[s_i * seg:(s_i + 1) * seg]
            if s_i > 0:
                pad_scr[base:base + CONV_PAD, n:n + N_CHUNK] = u[s_i * seg - CONV_PAD:s_i * seg] * joined
            if s_i < n_seg - 1:
                pad_scr[base + CONV_PAD + seg:base + stride, n:n + N_CHUNK] = (
                    u[(s_i + 1) * seg:(s_i + 1) * seg + CONV_PAD] * joined)
    lanes = 256
    pad_rows = n_seg * stride
    for n in range(0, d, lanes):
        blk = pad_scr[:, n:n + lanes]
        for s in range(1, SUBLANES):
            sh_scr[s - 1, :, n:n + lanes] = pltpu.roll(blk, pad_rows - s, axis=0)
    blocks = [range(s0, s0 + CV_SEGS) for s0 in range(0, n_seg, CV_SEGS)]
    x_new = []
    for segs in blocks:
        for s_i in segs:
            for n in range(0, d, lanes):
                acc = jnp.zeros((seg, lanes), F32) + bdw_ref[:, n:n + lanes]
                for k in range(width):
                    whole, phase = divmod(CONV_PAD + k - half, SUBLANES)
                    row0 = s_i * stride + whole * SUBLANES
                    if phase == 0:
                        src = pad_scr[row0:row0 + seg, n:n + lanes]
                    else:
                        src = sh_scr[phase - 1, row0:row0 + seg, n:n + lanes]
                    acc = acc + wdw_ref[k:k + 1, n:n + lanes] * src
                cv_scr[s_i * seg:(s_i + 1) * seg, n:n + lanes] = acc
        rs = slice(segs[0] * seg, (segs[-1] + 1) * seg)
        cv = cv_scr[rs, :]
        mu = jnp.mean(cv, axis=-1, keepdims=True)
        xc = cv - mu
        var = jnp.mean(xc * xc, axis=-1, keepdims=True)
        ln = xc * lax.rsqrt(var + EPS) * lng_ref[...] + lnb_ref[...]
        out = _dot(_silu(ln), w2_ref[...]) + b2_ref[...]
        x_new.append(x_scr[rs, :] + mod[2:3] * out)
    _moe_pre_init(cnt_scr)
    for segs, xb in zip(blocks, x_new):
        r0 = segs[0] * seg
        xo_ref[r0:r0 + xb.shape[0], :] = xb
        _moe_pre(xb, r0, mod, g2_ref[...], wr_ref[...], br_ref[...], h2_ref, topi_ref, gate_ref, rank_ref, cnt_ref,
                 cnt_scr)


def _conformer(pending, mod_i, g1, w_pw1, b_pw1, w_dw, b_dw, ln_g, ln_b, w_pw2, b_pw2, g2, w_router, b_router,
               with_ctx):
    x_mid, mod_prev, gates_t, picked = pending
    bsz, s, d = x_mid.shape
    t0 = 0 if with_ctx else 1
    nt_in = s // TILE
    nt = nt_in - t0
    ctx_row = bsz
    n_exp = w_router.shape[1]
    pad_rows = (TILE // GRID_W) * (GRID_W + 2 * CONV_PAD)
    pre_in, pre_out = _moe_pre_specs(d, n_exp)
    row = lambda v: v.astype(F32).reshape(1, -1)
    mod_spec = pl.BlockSpec((None, 6, d), lambda b, t: (jnp.where(t + t0 == 0, ctx_row, b), 0, 0))
    flat = pl.BlockSpec((TILE, d), lambda b, t: (b * nt_in + t + t0, 0))
    return pl.pallas_call(
        functools.partial(_cv_kernel, t0),
        grid=(bsz, nt),
        in_specs=[
            pl.BlockSpec((None, TILE, d), lambda b, t: (b, t + t0, 0)),
            mod_spec,
            pl.BlockSpec((None, TILE, TOP_K), lambda b, t: (b, t + t0, 0)),
            flat, flat, flat, flat,
            mod_spec,
            _full_spec((1, d)),
            _full_spec((d, 2 * d)), _full_spec((1, 2 * d)),
            _full_spec(w_dw.shape), _full_spec((1, d)), _full_spec((1, d)), _full_spec((1, d)),
            _full_spec((d, d)), _full_spec((1, d)),
        ] + pre_in,
        out_specs=pre_out,
        out_shape=_moe_pre_shapes(bsz, nt * TILE, d, n_exp),
        scratch_shapes=[
            pltpu.VMEM((pad_rows, d), F32),
            pltpu.VMEM((SUBLANES - 1, pad_rows, d), F32),
            pltpu.VMEM((TILE, d), F32),
            _moe_pre_scratch(n_exp),
            pltpu.VMEM((TILE, d), F32),
        ],
        compiler_params=_cparams(2, 56),
        name="conformer",
    )(x_mid, mod_prev, gates_t, *picked, mod_i,
      row(g1), w_pw1.astype(MXU_DTYPE), row(b_pw1), w_dw.astype(F32), row(b_dw), row(ln_g), row(ln_b),
      w_pw2.astype(MXU_DTYPE), row(b_pw2), row(g2), w_router.T.astype(F32),
      b_router.astype(F32).reshape(n_exp, 1))


def _moe_kernel(be_ref, flag_ref, x_ref, wgu_ref, bgu_ref, wdn_ref, bdn_ref, y_ref, wgu_scr, wdn_scr, act_scr):
    i = pl.program_id(0)
    flags = flag_ref[i]
    d_ff = wdn_ref.shape[0]

    @pl.when((flags & 2) != 0)
    def _():
        for n in range(0, 2 * d_ff, N_CHUNK):
            wgu_scr[:, n:n + N_CHUNK] = wgu_ref[:, n:n + N_CHUNK].astype(wgu_scr.dtype)
        for n in range(0, wdn_ref.shape[1], N_CHUNK):
            wdn_scr[:, n:n + N_CHUNK] = wdn_ref[:, n:n + N_CHUNK].astype(wdn_scr.dtype)

    @pl.when((flags & 1) != 0)
    def _():
        x = x_ref[...]
        for n in range(0, d_ff, N_CHUNK):
            gate = _dot(x, wgu_scr[:, n:n + N_CHUNK]) + bgu_ref[:, n:n + N_CHUNK]
            up = _dot(x, wgu_scr[:, d_ff + n:d_ff + n + N_CHUNK]) + bgu_ref[:, d_ff + n:d_ff + n + N_CHUNK]
            gate = jnp.minimum(gate, SWIGLU_LIMIT)
            up = jnp.clip(up, -SWIGLU_LIMIT, SWIGLU_LIMIT)
            glu = gate * _sigmoid(gate * SWIGLU_ALPHA)
            act_scr[:, n:n + N_CHUNK] = ((up + 1.0) * glu).astype(act_scr.dtype)
        act = act_scr[...]
        for n in range(0, y_ref.shape[1], N_CHUNK):
            y_ref[:, n:n + N_CHUNK] = (_dot(act, wdn_scr[:, n:n + N_CHUNK])
                                       + bdn_ref[:, n:n + N_CHUNK]).astype(y_ref.dtype)

    @pl.when((flags & 1) == 0)
    def _():
        y_ref[...] = jnp.zeros_like(y_ref)


def _moe_experts(xg, block_expert, block_flags, layer, w_gu, b_gu, w_down, b_down):
    n_rows, d = xg.shape
    depth, n_exp, _, d_gu = w_gu.shape
    d_ff = w_down.shape[2]
    nb = n_rows // MOE_TM
    grid_spec = pltpu.PrefetchScalarGridSpec(
        num_scalar_prefetch=2,
        grid=(nb,),
        in_specs=[
            pl.BlockSpec((MOE_TM, d), lambda i, be, fl: (i, 0)),
            pl.BlockSpec((None, None, d, d_gu), lambda i, be, fl: (layer, be[i], 0, 0)),
            pl.BlockSpec((None, None, 1, d_gu), lambda i, be, fl: (layer, be[i], 0, 0)),
            pl.BlockSpec((None, None, d_ff, d), lambda i, be, fl: (layer, be[i], 0, 0)),
            pl.BlockSpec((None, None, 1, d), lambda i, be, fl: (layer, be[i], 0, 0)),
        ],
        out_specs=pl.BlockSpec((MOE_TM, d), lambda i, be, fl: (i, 0)),
        scratch_shapes=[
            pltpu.VMEM((d, d_gu), MXU_DTYPE),
            pltpu.VMEM((d_ff, d), MXU_DTYPE),
            pltpu.VMEM((MOE_TM, d_ff), MXU_DTYPE),
        ],
    )
    return pl.pallas_call(
        _moe_kernel,
        grid_spec=grid_spec,
        out_shape=jax.ShapeDtypeStruct((n_rows, d), ACT_DTYPE),
        compiler_params=_cparams(1, 56),
        name="moe_experts",
    )(block_expert, block_flags, xg, w_gu, b_gu.reshape(depth, n_exp, 1, d_gu), w_down,
      b_down.reshape(depth, n_exp, 1, d))


def _take(rows, idx):
    return rows.at[idx].get(mode="promise_in_bounds")


def _route(topi, rank, counts):
    n_exp = counts.shape[0]
    n_assign = topi.size
    experts = jnp.arange(n_exp, dtype=jnp.int32)
    padded = (counts + MOE_TM - 1) // MOE_TM * MOE_TM
    pad_end = jnp.cumsum(padded)
    pad_start = pad_end - padded
    unpad_start = jnp.cumsum(counts) - counts
    start_of = jnp.sum(jnp.where(topi[..., None] == experts, pad_start, 0), axis=-1)
    slot_of_assign = start_of + rank
    nb = -(-n_assign // MOE_TM) + n_exp
    block_start = jnp.arange(nb, dtype=jnp.int32) * MOE_TM
    valid = block_start < pad_end[-1]
    be = jnp.minimum(jnp.sum((block_start[:, None] >= pad_end[None, :]).astype(jnp.int32), axis=1), n_exp - 1)
    n_valid = pad_end[-1] // MOE_TM
    be = jnp.where(valid, be, be[jnp.maximum(n_valid - 1, 0)])
    first = jnp.logical_and(valid, block_start == pad_start[be])
    flags = valid.astype(jnp.int32) + 2 * first.astype(jnp.int32)
    by_slot = jnp.argsort(slot_of_assign.reshape(-1)).astype(jnp.int32)
    n_tok = topi.shape[1]
    slot = jnp.arange(nb * MOE_TM, dtype=jnp.int32)
    per_slot = lambda per_block: jnp.repeat(per_block, MOE_TM)
    r_slot = slot - per_slot(pad_start[be])
    occupied = jnp.logical_and(per_slot(valid), r_slot < per_slot(counts[be]))
    src = jnp.clip(per_slot(unpad_start[be]) + r_slot, 0, n_assign - 1)
    tok_of_slot = jnp.where(occupied, _take(by_slot, src) % n_tok, slot % n_tok)
    return tok_of_slot, slot_of_assign, be, flags


def _combine_kernel(final, x_ref, mod_ref, g_ref, p0_ref, p1_ref, p2_ref, p3_ref, fg_ref, o_ref):
    x = _moe_combined(x_ref, mod_ref, g_ref, (p0_ref, p1_ref, p2_ref, p3_ref))
    if final:
        ms = jnp.mean(x * x, axis=-1, keepdims=True)
        x = x * lax.rsqrt(ms + EPS) * fg_ref[...]
    o_ref[...] = x


def _combine(pending, final_g, with_ctx, final):
    x_mid, mod_i, gates_t, picked = pending
    bsz, s, d = x_mid.shape
    nt = s // TILE
    ctx_row = bsz
    mod_row = (lambda b, t: jnp.where(t == 0, ctx_row, b)) if with_ctx else (lambda b, t: b)
    tok = pl.BlockSpec((None, TILE, d), lambda b, t: (b, t, 0))
    flat = pl.BlockSpec((TILE, d), lambda b, t: (b * nt + t, 0))
    return pl.pallas_call(
        functools.partial(_combine_kernel, final),
        grid=(bsz, nt),
        in_specs=[
            tok,
            pl.BlockSpec((None, 6, d), lambda b, t: (mod_row(b, t), 0, 0)),
            pl.BlockSpec((None, TILE, TOP_K), lambda b, t: (b, t, 0)),
            flat, flat, flat, flat,
            _full_spec((1, d)),
        ],
        out_specs=tok,
        out_shape=jax.ShapeDtypeStruct((bsz, s, d), F32),
        compiler_params=_cparams(2, 32),
        name="moe_combine",
    )(x_mid, mod_i, gates_t, *picked, final_g.astype(F32).reshape(1, d))


def _moe(x_mid, mod_i, h2, topi, gates, rank, counts, layer, w_gu, b_gu, w_down, b_down):
    bsz, s, d = h2.shape
    flat = lambda v: v.transpose(1, 0, 2).reshape(TOP_K, bsz * s)
    tok_of_slot, slot_of_assign, be, flags = _route(flat(topi), flat(rank), counts[:, 0].astype(jnp.int32))
    xg = _take(h2.reshape(bsz * s, d), tok_of_slot)
    y = _moe_experts(xg, be, flags, layer, w_gu, b_gu, w_down, b_down)
    picked = [_take(y, slot_of_assign[k]) for k in range(TOP_K)]
    return x_mid, mod_i, gates.transpose(0, 2, 1), picked


def kernel(x, c, ctx, c_ctx, w_ada, b_ada, norm1_g, norm2_g, ssd_w_in, ssd_w_conv, ssd_b_conv, ssd_dt_bias, ssd_a_log, ssd_d, ssd_norm_g, ssd_w_out, cv_w_pw1, cv_b_pw1, cv_w_dw, cv_b_dw, cv_ln_g, cv_ln_b, cv_w_pw2, cv_b_pw2, moe_w_router, moe_b_router, moe_w_gu, moe_b_gu, moe_w_down, moe_b_down, final_g):
    bsz, seq, d = x.shape
    n_ctx = ctx.shape[1]
    depth = w_ada.shape[0]
    assert n_ctx == TILE and seq % TILE == 0 and TILE % GRID_W == 0 and TILE % SSD_CHUNK == 0
    assert depth % N_MIXERS == 0

    mod_rows = -(-(bsz + 1) // 8) * 8
    c_all = jnp.zeros((mod_rows, d), F32).at[:bsz].set(c).at[bsz].set(c_ctx)
    mods = _ada_mods(c_all, w_ada, b_ada)
    xs = jnp.concatenate([ctx, x], axis=1)

    pending = None
    for i in range(depth):
        j = i // N_MIXERS
        use_ssd = (i % N_MIXERS) == 0
        need_ctx = i < depth - 1
        mod_i = mods[i]
        if use_ssd:
            if pending is not None:
                xs = _combine(pending, final_g, True, False)
            z_g, xbc, acum, acum_t = _ssd_inproj(xs, mod_i, norm1_g[i], ssd_w_in[j], ssd_dt_bias[j], ssd_a_log[j])
            xs_g, bmt_g, cm_g = _ssd_conv(xbc, ssd_w_conv[j], ssd_b_conv[j])
            s = xs.shape[1]
            acum_d = acum.reshape(bsz, s, 2, N_SSD_HEADS).transpose(2, 0, 1, 3)
            acum_t_d = acum_t.reshape(bsz, 2, N_SSD_HEADS, s).transpose(1, 0, 2, 3)
            y = _ssd_scan(xs_g, bmt_g, cm_g, acum_d, acum_t_d, n_ctx // SSD_CHUNK)
            x_mid, h2, topi, gates, rank, counts = _ssd_out(
                xs, mod_i, y, xs_g, z_g, ssd_d[j], ssd_norm_g[j], ssd_w_out[j], norm2_g[i], moe_w_router[i],
                moe_b_router[i])
        else:
            x_mid, h2, topi, gates, rank, counts = _conformer(
                pending, mod_i, norm1_g[i], cv_w_pw1[j], cv_b_pw1[j], cv_w_dw[j], cv_b_dw[j], cv_ln_g[j],
                cv_ln_b[j], cv_w_pw2[j], cv_b_pw2[j], norm2_g[i], moe_w_router[i], moe_b_router[i], need_ctx)
        pending = _moe(x_mid, mod_i, h2, topi, gates, rank, counts, i, moe_w_gu, moe_b_gu, moe_w_down, moe_b_down)
    return _combine(pending, final_g, False, True)
```

```python
import functools

import jax
import jax.numpy as jnp
from jax import lax
from jax.experimental import pallas as pl
from jax.experimental.pallas import tpu as pltpu

F32 = jnp.float32
BF16 = jnp.bfloat16
MXU_DTYPE = BF16
ACT_DTYPE = BF16

EPS = 1e-6
LOG2_E = 1.4426950408889634
GRID_W = 64
N_MIXERS = 2
HEAD_DIM = 64
N_SSD_HEADS = 32
N_SSD_GROUPS = 8
HEADS_PER_GROUP = N_SSD_HEADS // N_SSD_GROUPS
D_STATE = 128
SSD_CHUNK = 128
GROUP_W = HEADS_PER_GROUP * HEAD_DIM
D_INNER = N_SSD_HEADS * HEAD_DIM
D_BC = N_SSD_GROUPS * D_STATE
D_XBC = D_INNER + 2 * D_BC
TOP_K = 4
SWIGLU_ALPHA = 1.702
SWIGLU_LIMIT = 7.0

TILE = 256
CONV_PAD = 16
SSD_HALO = 8
N_CHUNK = 512
MOE_TM = 512
CNT_LANES = 128
SUBLANES = 8
OUT_ROWS = 128
CV_SEGS = 2
VMEM_MB = 1024 * 1024


def _cparams(n_axes, vmem_mb):
    return pltpu.CompilerParams(dimension_semantics=("arbitrary",) * n_axes,
                                vmem_limit_bytes=vmem_mb * VMEM_MB)


def _full_spec(shape):
    zeros = (0,) * len(shape)
    return pl.BlockSpec(shape, lambda *_: zeros)


def _dot(a, b):
    return jnp.dot(a.astype(MXU_DTYPE), b.astype(MXU_DTYPE), preferred_element_type=F32)


def _dot_nt(a, b):
    return lax.dot_general(a.astype(MXU_DTYPE), b.astype(MXU_DTYPE), (((1,), (1,)), ((), ())),
                           preferred_element_type=F32)


def _split3(v):
    hi = v.astype(BF16)
    r1 = v - hi.astype(F32)
    mid = r1.astype(BF16)
    lo = (r1 - mid.astype(F32)).astype(BF16)
    return hi, mid, lo


def _dot01_left(m01, v):
    hi, mid, lo = _split3(v)
    m = m01.astype(BF16)
    d = lambda p: jnp.dot(m, p, preferred_element_type=F32)
    return d(hi) + d(mid) + d(lo)


def _dot01_right(v, m01):
    hi, mid, lo = _split3(v)
    m = m01.astype(BF16)
    d = lambda p: jnp.dot(p, m, preferred_element_type=F32)
    return d(hi) + d(mid) + d(lo)


def _dot_nt_precise(a, b):
    ah = a.astype(BF16)
    al = (a - ah.astype(F32)).astype(BF16)
    bh = b.astype(BF16)
    bl = (b - bh.astype(F32)).astype(BF16)
    d = lambda p, q: lax.dot_general(p, q, (((1,), (1,)), ((), ())), preferred_element_type=F32)
    return d(ah, bh) + d(ah, bl) + d(al, bh)


def _sigmoid(v):
    return 1.0 / (1.0 + jnp.exp(-v))


def _silu(v):
    return v * _sigmoid(v)


def _softplus(v):
    return jnp.maximum(v, 0.0) + jnp.log(1.0 + jnp.exp(-jnp.abs(v)))


def _rms_mod(x, g, shift, scale):
    ms = jnp.mean(x * x, axis=-1, keepdims=True)
    y = x * lax.rsqrt(ms + EPS) * g
    return y * (1.0 + scale) + shift


def _ada_kernel(c_ref, w_ref, b_ref, o_ref):
    o_ref[...] = _dot(_silu(c_ref[...]), w_ref[...]) + b_ref[...]


def _ada_mods(c_all, w_ada, b_ada):
    depth, d, d6 = w_ada.shape
    n_mod = d6 // d
    rows = c_all.shape[0]
    out = pl.pallas_call(
        _ada_kernel,
        grid=(depth, n_mod),
        in_specs=[
            pl.BlockSpec((rows, d), lambda i, n: (0, 0)),
            pl.BlockSpec((None, d, d), lambda i, n: (i, 0, n)),
            pl.BlockSpec((None, None, 1, d), lambda i, n: (i, n, 0, 0)),
        ],
        out_specs=pl.BlockSpec((None, None, rows, d), lambda i, n: (i, n, 0, 0)),
        out_shape=jax.ShapeDtypeStruct((depth, n_mod, rows, d), F32),
        compiler_params=_cparams(2, 32),
        name="ada_mods",
    )(c_all, w_ada, b_ada.reshape(depth, n_mod, 1, d))
    return out.transpose(0, 2, 1, 3)


def _moe_pre_init(cnt_scr):
    @pl.when(jnp.logical_and(pl.program_id(0) == 0, pl.program_id(1) == 0))
    def _():
        cnt_scr[...] = jnp.zeros_like(cnt_scr)


def _moe_pre(x_new, r0, mod, g2, wr_t, b_r, h2_ref, topi_ref, gate_ref, rank_ref, cnt_ref, cnt_scr):
    rows = x_new.shape[0]
    rs = slice(r0, r0 + rows)
    h2 = _rms_mod(x_new, g2, mod[3:4], mod[4:5])
    h2_ref[rs, :] = h2.astype(h2_ref.dtype)
    logits = _dot_nt_precise(wr_t, h2) + b_r
    n_exp = logits.shape[0]
    eidx = lax.broadcasted_iota(jnp.int32, logits.shape, 0)
    vals, idxs = [], []
    cur = logits
    for _ in range(TOP_K):
        m = jnp.max(cur, axis=0, keepdims=True)
        idx = jnp.min(jnp.where(cur == m, eidx, n_exp), axis=0, keepdims=True)
        vals.append(m)
        idxs.append(idx)
        cur = jnp.where(eidx == idx, -jnp.inf, cur)
    es = [jnp.exp(v - vals[0]) for v in vals]
    tot = es[0] + es[1] + es[2] + es[3]
    t_src = lax.broadcasted_iota(jnp.int32, (rows, rows), 0)
    t_dst = lax.broadcasted_iota(jnp.int32, (rows, rows), 1)
    earlier = jnp.where(t_src < t_dst, 1.0, 0.0).astype(BF16)
    base = cnt_scr[:, 0:1]
    onehots = [jnp.where(eidx == idxs[k], 1.0, 0.0) for k in range(TOP_K)]
    chosen = onehots[0] + onehots[1] + onehots[2] + onehots[3]
    before = base + jnp.dot(chosen.astype(BF16), earlier, preferred_element_type=F32)
    for k in range(TOP_K):
        topi_ref[k:k + 1, rs] = idxs[k]
        gate_ref[k:k + 1, rs] = es[k] / tot
        rank_ref[k:k + 1, rs] = jnp.sum(onehots[k] * before, axis=0, keepdims=True).astype(jnp.int32)
    totals = jnp.broadcast_to(base + jnp.sum(chosen, axis=1, keepdims=True), cnt_scr.shape)
    cnt_scr[...] = totals
    cnt_ref[...] = totals


def _moe_pre_specs(d, n_exp):
    in_specs = [_full_spec((1, d)), _full_spec((n_exp, d)), _full_spec((n_exp, 1))]
    out_specs = [
        pl.BlockSpec((None, TILE, d), lambda b, t: (b, t, 0)),
        pl.BlockSpec((None, TILE, d), lambda b, t: (b, t, 0)),
        pl.BlockSpec((None, TOP_K, TILE), lambda b, t: (b, 0, t)),
        pl.BlockSpec((None, TOP_K, TILE), lambda b, t: (b, 0, t)),
        pl.BlockSpec((None, TOP_K, TILE), lambda b, t: (b, 0, t)),
        _full_spec((n_exp, CNT_LANES)),
    ]
    return in_specs, out_specs


def _moe_pre_shapes(bsz, s_out, d, n_exp):
    return [
        jax.ShapeDtypeStruct((bsz, s_out, d), F32),
        jax.ShapeDtypeStruct((bsz, s_out, d), ACT_DTYPE),
        jax.ShapeDtypeStruct((bsz, TOP_K, s_out), jnp.int32),
        jax.ShapeDtypeStruct((bsz, TOP_K, s_out), F32),
        jax.ShapeDtypeStruct((bsz, TOP_K, s_out), jnp.int32),
        jax.ShapeDtypeStruct((n_exp, CNT_LANES), F32),
    ]


def _moe_pre_scratch(n_exp):
    return pltpu.VMEM((n_exp, CNT_LANES), F32)


def _inproj_kernel(x_ref, mod_ref, g_ref, wz_ref, wx_ref, wdt_ref, wdt_t_ref, dtb_ref, dtb_t_ref,
                   a_ref, a_t_ref, z_ref, xbc_ref, acum_ref, acum_t_ref):
    mod = mod_ref[...]
    h = _rms_mod(x_ref[...], g_ref[...], mod[0:1], mod[1:2]).astype(MXU_DTYPE)
    for g in range(N_SSD_GROUPS):
        z_ref[g] = _dot(h, wz_ref[:, g * GROUP_W:(g + 1) * GROUP_W]).astype(z_ref.dtype)
    for n in range(0, D_XBC, N_CHUNK):
        xbc_ref[:, n:n + N_CHUNK] = _dot(h, wx_ref[:, n:n + N_CHUNK]).astype(xbc_ref.dtype)
    dt = _softplus(_dot(h, wdt_ref[...]) + dtb_ref[...])
    dt_t = _softplus(_dot_nt(wdt_t_ref[...], h) + dtb_t_ref[...])
    dta = dt * a_ref[...]
    dta_t = dt_t * a_t_ref[...]
    ii = lax.broadcasted_iota(jnp.int32, (TILE, TILE), 0)
    jj = lax.broadcasted_iota(jnp.int32, (TILE, TILE), 1)
    same = (ii // SSD_CHUNK) == (jj // SSD_CHUNK)
    lower = jnp.where(same, jnp.where(jj <= ii, 1.0, 0.0), 0.0)
    upper = jnp.where(same, jnp.where(jj >= ii, 1.0, 0.0), 0.0)
    h_n = N_SSD_HEADS
    acum_ref[0] = _dot01_left(lower, dta[:, :h_n])
    acum_ref[1] = _dot01_left(upper, dta[:, h_n:])
    log_dt = jnp.log(dt_t) * LOG2_E
    acum_t_ref[0] = _dot01_right(dta_t[:h_n], upper) - log_dt[:h_n]
    acum_t_ref[1] = _dot01_right(dta_t[h_n:], lower) - log_dt[h_n:]


def _ssd_inproj(xs, mod_i, g1, w_in, dt_bias, a_log):
    bsz, s, d = xs.shape
    nt = s // TILE
    ctx_row = bsz
    wz = w_in[:, :D_INNER].astype(MXU_DTYPE)
    wx = w_in[:, D_INNER:D_INNER + D_XBC].astype(MXU_DTYPE)
    wdt = w_in[:, D_INNER + D_XBC:].astype(MXU_DTYPE)
    n_dt = 2 * N_SSD_HEADS
    a = -jnp.exp(a_log.astype(F32)).reshape(1, n_dt) * LOG2_E
    dtb = dt_bias.astype(F32).reshape(1, n_dt)
    tok = lambda w: pl.BlockSpec((None, TILE, w), lambda b, t: (b, t, 0))
    return pl.pallas_call(
        _inproj_kernel,
        grid=(bsz, nt),
        in_specs=[
            tok(d),
            pl.BlockSpec((None, 6, d), lambda b, t: (jnp.where(t == 0, ctx_row, b), 0, 0)),
            _full_spec((1, d)),
            _full_spec((d, D_INNER)), _full_spec((d, D_XBC)), _full_spec((d, n_dt)), _full_spec((n_dt, d)),
            _full_spec((1, n_dt)), _full_spec((n_dt, 1)), _full_spec((1, n_dt)), _full_spec((n_dt, 1)),
        ],
        out_specs=[
            pl.BlockSpec((None, N_SSD_GROUPS, TILE, GROUP_W), lambda b, t: (b, 0, t, 0)),
            tok(D_XBC),
            pl.BlockSpec((2, None, TILE, N_SSD_HEADS), lambda b, t: (0, b, t, 0)),
            pl.BlockSpec((2, None, N_SSD_HEADS, TILE), lambda b, t: (0, b, 0, t)),
        ],
        out_shape=[
            jax.ShapeDtypeStruct((bsz, N_SSD_GROUPS, s, GROUP_W), ACT_DTYPE),
            jax.ShapeDtypeStruct((bsz, s, D_XBC), ACT_DTYPE),
            jax.ShapeDtypeStruct((2, bsz, s, N_SSD_HEADS), F32),
            jax.ShapeDtypeStruct((2, bsz, N_SSD_HEADS, s), F32),
        ],
        compiler_params=_cparams(2, 56),
        name="ssd_inproj",
    )(xs, mod_i, g1.reshape(1, d), wz, wx, wdt, wdt.T, dtb, dtb.T, a, a.T)


def _ssd_conv_kernel(nt, main_ref, prev_ref, next_ref, w_ref, b_ref, xs_ref, bmt_ref, cm_ref, shifted_scr, bm_scr):
    t = pl.program_id(1)
    width = w_ref.shape[0]
    half = (width - 1) // 2
    assert half <= SSD_HALO
    prev_ok = jnp.where(t >= 2, 1.0, 0.0)
    next_ok = jnp.where(jnp.logical_and(t >= 1, t < nt - 1), 1.0, 0.0)
    taps = [k for k in range(width) if k != half]
    r_out = lax.broadcasted_iota(jnp.int32, (TILE, TILE), 0)
    r_in = lax.broadcasted_iota(jnp.int32, (TILE, TILE), 1)
    shifts = jnp.concatenate(
        [jnp.where(r_in - r_out == k - half, 1.0, 0.0).astype(main_ref.dtype) for k in taps], axis=0)
    halo_row = lax.broadcasted_iota(jnp.int32, (SSD_HALO, 1), 0)
    rows = 64
    for n in range(0, D_XBC, N_CHUNK):
        cols = slice(n, n + N_CHUNK)
        u = main_ref[:, cols]
        shifted_scr[...] = jnp.dot(shifts, u, preferred_element_type=F32)
        pv = prev_ref[:, cols].astype(F32) * prev_ok
        nx = next_ref[:, cols].astype(F32) * next_ok
        head = jnp.zeros((SSD_HALO, N_CHUNK), F32)
        tail = jnp.zeros((SSD_HALO, N_CHUNK), F32)
        for k in range(width):
            m = abs(k - half)
            if k < half:
                head = head + jnp.where(halo_row < m, w_ref[k:k + 1, cols] * pltpu.roll(pv, m, axis=0), 0.0)
            elif k > half:
                tail = tail + jnp.where(halo_row >= SSD_HALO - m,
                                        w_ref[k:k + 1, cols] * pltpu.roll(nx, SSD_HALO - m, axis=0), 0.0)
        for r0 in range(0, TILE, rows):
            acc = b_ref[:, cols] + w_ref[half:half + 1, cols] * u[r0:r0 + rows].astype(F32)
            for i, k in enumerate(taps):
                acc = acc + w_ref[k:k + 1, cols] * shifted_scr[i * TILE + r0:i * TILE + r0 + rows, :]
            if r0 == 0:
                acc = jnp.concatenate([acc[:SSD_HALO] + head, acc[SSD_HALO:]], axis=0)
            if r0 == TILE - rows:
                acc = jnp.concatenate([acc[:rows - SSD_HALO], acc[rows - SSD_HALO:] + tail], axis=0)
            out = _silu(acc)
            for c0 in range(n, n + N_CHUNK, D_STATE):
                piece = out[:, c0 - n:c0 - n + D_STATE]
                if c0 < D_INNER:
                    g, o = divmod(c0, GROUP_W)
                    xs_ref[g, r0:r0 + rows, o:o + D_STATE] = piece.astype(xs_ref.dtype)
                elif c0 < D_INNER + D_BC:
                    bm_scr[r0:r0 + rows, c0 - D_INNER:c0 - D_INNER + D_STATE] = piece
                else:
                    cm_ref[(c0 - D_INNER - D_BC) // D_STATE, r0:r0 + rows, :] = piece.astype(cm_ref.dtype)
    for g in range(N_SSD_GROUPS):
        for r0 in range(0, TILE, D_STATE):
            blk = bm_scr[r0:r0 + D_STATE, g * D_STATE:(g + 1) * D_STATE]
            bmt_ref[g, :, r0:r0 + D_STATE] = blk.T.astype(bmt_ref.dtype)


def _ssd_conv(xbc, w_conv, b_conv):
    bsz, s, c = xbc.shape
    nt = s // TILE
    per_tile = TILE // SSD_HALO
    last_halo = s // SSD_HALO - 1
    grp = lambda w: pl.BlockSpec((None, N_SSD_GROUPS, TILE, w), lambda b, t: (b, 0, t, 0))
    return pl.pallas_call(
        functools.partial(_ssd_conv_kernel, nt),
        grid=(bsz, nt),
        in_specs=[
            pl.BlockSpec((None, TILE, c), lambda b, t: (b, t, 0)),
            pl.BlockSpec((None, SSD_HALO, c), lambda b, t: (b, jnp.maximum(t * per_tile - 1, 0), 0)),
            pl.BlockSpec((None, SSD_HALO, c), lambda b, t: (b, jnp.minimum((t + 1) * per_tile, last_halo), 0)),
            _full_spec(w_conv.shape), _full_spec((1, c)),
        ],
        out_specs=[
            grp(GROUP_W),
            pl.BlockSpec((None, N_SSD_GROUPS, D_STATE, TILE), lambda b, t: (b, 0, 0, t)),
            grp(D_STATE),
        ],
        out_shape=[
            jax.ShapeDtypeStruct((bsz, N_SSD_GROUPS, s, GROUP_W), ACT_DTYPE),
            jax.ShapeDtypeStruct((bsz, N_SSD_GROUPS, D_STATE, s), ACT_DTYPE),
            jax.ShapeDtypeStruct((bsz, N_SSD_GROUPS, s, D_STATE), ACT_DTYPE),
        ],
        scratch_shapes=[
            pltpu.VMEM(((w_conv.shape[0] - 1) * TILE, N_CHUNK), F32),
            pltpu.VMEM((TILE, D_BC), F32),
        ],
        compiler_params=_cparams(2, 48),
        name="ssd_conv",
    )(xbc, xbc, xbc, w_conv.astype(F32), b_conv.astype(F32).reshape(1, c))


def _ssd_scan_kernel(xs_ref, bmt_ref, cm_ref, acol_ref, arow_ref, y_ref, st_ref):
    d = pl.program_id(0)
    c = pl.program_id(2)

    @pl.when(c == 0)
    def _():
        st_ref[...] = jnp.zeros_like(st_ref)

    q = SSD_CHUNK
    ii = lax.broadcasted_iota(jnp.int32, (q, q), 0)
    jj = lax.broadcasted_iota(jnp.int32, (q, q), 1)
    mask = (ii - jj) * (1 - 2 * d) >= 0
    pair_w = 2 * HEAD_DIM
    first_of_pair = lax.broadcasted_iota(jnp.int32, (1, pair_w), 1) < HEAD_DIM
    def scores_of(g):
        return _dot(cm_ref[g], bmt_ref[g])

    def factors_of(g, scores):
        hs = slice(g * HEADS_PER_GROUP, (g + 1) * HEADS_PER_GROUP)
        bmt_f = bmt_ref[g].astype(F32)
        cm_f = cm_ref[g].astype(F32)
        acol = acol_ref[:, hs]
        arow = arow_ref[hs, :]
        a_tot = jnp.where(d == 0, acol[q - 1:q, :], acol[0:1, :])
        lhs, new = [], []
        for r in range(HEADS_PER_GROUP):
            a_i = jnp.broadcast_to(acol[:, r:r + 1], (q, q))
            within = scores * jnp.where(mask, jnp.exp2(a_i - arow[r:r + 1, :]), 0.0)
            carried = cm_f * jnp.exp2(a_i)
            lhs.append(jnp.concatenate([within.astype(MXU_DTYPE), carried.astype(MXU_DTYPE)], axis=1))
            new.append((bmt_f * jnp.exp2(a_tot[:, r:r + 1] - arow[r:r + 1, :])).astype(MXU_DTYPE))
        return lhs, new, a_tot

    ahead = [scores_of(0)]
    for g in range(N_SSD_GROUPS):
        if g + 1 < N_SSD_GROUPS:
            ahead.append(scores_of(g + 1))
        lhs, new, a_tot = factors_of(g, ahead[g])
        x = xs_ref[g].astype(MXU_DTYPE)
        st = st_ref[g]
        st_m = st.astype(MXU_DTYPE)
        for p in range(HEADS_PER_GROUP // 2):
            lanes = slice(p * pair_w, (p + 1) * pair_w)
            x_pair = x[:, lanes]
            rhs = jnp.concatenate([x_pair, st_m[:, lanes]], axis=0)
            both = _dot(jnp.concatenate([lhs[2 * p], lhs[2 * p + 1]], axis=0), rhs)
            y_pair = jnp.where(first_of_pair, both[:q], both[q:])
            y_ref[g, :, lanes] = y_pair.astype(y_ref.dtype)
            keep = jnp.exp2(jnp.where(first_of_pair, a_tot[:, 2 * p:2 * p + 1], a_tot[:, 2 * p + 1:2 * p + 2]))
            zero = jnp.zeros_like(x_pair)
            x_split = jnp.concatenate([jnp.where(first_of_pair, x_pair, zero),
                                       jnp.where(first_of_pair, zero, x_pair)], axis=0)
            b_both = jnp.concatenate([new[2 * p], new[2 * p + 1]], axis=1)
            st_ref[g, :, lanes] = st[:, lanes] * keep + _dot(b_both, x_split)


def _ssd_scan(xs_g, bmt_g, cm_g, acum_d, acum_t_d, n_ctx_chunks):
    bsz, n_grp, s, _ = xs_g.shape
    nc = s // SSD_CHUNK
    last = nc - 1 + n_ctx_chunks

    def chunk(d, c):
        back = jnp.where(c < n_ctx_chunks, n_ctx_chunks - 1 - c, last - c)
        return jnp.where(d == 0, c, back)

    big = lambda w: pl.BlockSpec((None, n_grp, SSD_CHUNK, w), lambda d, b, c: (b, 0, chunk(d, c), 0))
    col = pl.BlockSpec((None, None, SSD_CHUNK, N_SSD_HEADS), lambda d, b, c: (d, b, chunk(d, c), 0))
    return pl.pallas_call(
        _ssd_scan_kernel,
        grid=(2, bsz, nc),
        in_specs=[
            big(GROUP_W),
            pl.BlockSpec((None, n_grp, D_STATE, SSD_CHUNK), lambda d, b, c: (b, 0, 0, chunk(d, c))),
            big(D_STATE), col,
            pl.BlockSpec((None, None, N_SSD_HEADS, SSD_CHUNK), lambda d, b, c: (d, b, 0, chunk(d, c))),
        ],
        out_specs=pl.BlockSpec((None, None, n_grp, SSD_CHUNK, GROUP_W),
                               lambda d, b, c: (d, b, 0, chunk(d, c), 0)),
        out_shape=jax.ShapeDtypeStruct((2, bsz, n_grp, s, GROUP_W), ACT_DTYPE),
        scratch_shapes=[pltpu.VMEM((n_grp, D_STATE, GROUP_W), F32)],
        compiler_params=_cparams(3, 32),
        name="ssd_scan",
    )(xs_g, bmt_g, cm_g, acum_d, acum_t_d)


def _ssd_out_kernel(x_ref, mod_ref, yf_ref, yb_ref, xs_ref, z_ref, dsk_ref, ng_ref, wo_ref,
                    g2_ref, wr_ref, br_ref, xo_ref, h2_ref, topi_ref, gate_ref, rank_ref, cnt_ref,
                    yn_scr, cnt_scr):
    mod = mod_ref[...]
    _moe_pre_init(cnt_scr)
    blocks = [slice(r0, r0 + OUT_ROWS) for r0 in range(0, TILE, OUT_ROWS)]
    for rs in blocks:
        for g in range(N_SSD_GROUPS):
            y = (yf_ref[g, rs, :].astype(F32) + yb_ref[g, rs, :].astype(F32)
                 + dsk_ref[g] * xs_ref[g, rs, :].astype(F32))
            y = y * _silu(z_ref[g, rs, :].astype(F32))
            ms = jnp.mean(y * y, axis=-1, keepdims=True)
            yn_scr[rs, g * GROUP_W:(g + 1) * GROUP_W] = (y * lax.rsqrt(ms + EPS) * ng_ref[g]).astype(yn_scr.dtype)
    x_new = [x_ref[rs, :] + mod[2:3] * _dot(yn_scr[rs, :], wo_ref[...]) for rs in blocks]
    for rs, xb in zip(blocks, x_new):
        xo_ref[rs, :] = xb
        _moe_pre(xb, rs.start, mod, g2_ref[...], wr_ref[...], br_ref[...], h2_ref, topi_ref, gate_ref, rank_ref,
                 cnt_ref, cnt_scr)


def _ssd_out(xs, mod_i, y, xs_g, z_g, d_skip, norm_g, w_out, g2, w_router, b_router):
    bsz, s, d = xs.shape
    nt = s // TILE
    ctx_row = bsz
    n_exp = w_router.shape[1]
    grp = lambda: pl.BlockSpec((None, N_SSD_GROUPS, TILE, GROUP_W), lambda b, t: (b, 0, t, 0))
    ydir = lambda dd: pl.BlockSpec((None, None, N_SSD_GROUPS, TILE, GROUP_W), lambda b, t: (dd, b, 0, t, 0))
    pre_in, pre_out = _moe_pre_specs(d, n_exp)
    dsk = jnp.repeat(d_skip.astype(F32), HEAD_DIM).reshape(N_SSD_GROUPS, 1, GROUP_W)
    return pl.pallas_call(
        _ssd_out_kernel,
        grid=(bsz, nt),
        in_specs=[
            pl.BlockSpec((None, TILE, d), lambda b, t: (b, t, 0)),
            pl.BlockSpec((None, 6, d), lambda b, t: (jnp.where(t == 0, ctx_row, b), 0, 0)),
            ydir(0), ydir(1), grp(), grp(),
            _full_spec((N_SSD_GROUPS, 1, GROUP_W)), _full_spec((N_SSD_GROUPS, 1, GROUP_W)),
            _full_spec((D_INNER, d)),
        ] + pre_in,
        out_specs=pre_out,
        out_shape=_moe_pre_shapes(bsz, s, d, n_exp),
        scratch_shapes=[pltpu.VMEM((TILE, D_INNER), MXU_DTYPE), _moe_pre_scratch(n_exp)],
        compiler_params=_cparams(2, 48),
        name="ssd_out",
    )(xs, mod_i, y, y, xs_g, z_g, dsk, norm_g.astype(F32).reshape(N_SSD_GROUPS, 1, GROUP_W),
      w_out.astype(MXU_DTYPE), g2.reshape(1, d), w_router.T.astype(F32), b_router.astype(F32).reshape(n_exp, 1))


def _moe_combined(x_ref, mod_ref, g_ref, p_refs):
    g = g_ref[...]
    f = g[:, 0:1] * p_refs[0][...].astype(F32)
    for k in range(1, TOP_K):
        f = f + g[:, k:k + 1] * p_refs[k][...].astype(F32)
    return x_ref[...] + mod_ref[5:6, :] * f


def _cv_kernel(t0, xm_ref, pmod_ref, pg_ref, p0_ref, p1_ref, p2_ref, p3_ref,
               mod_ref, g1_ref, w1_ref, b1_ref, wdw_ref, bdw_ref, lng_ref, lnb_ref, w2_ref, b2_ref,
               g2_ref, wr_ref, br_ref, xo_ref, h2_ref, topi_ref, gate_ref, rank_ref, cnt_ref,
               pad_scr, sh_scr, cv_scr, cnt_scr, x_scr):
    tile = pl.program_id(1) + t0
    d = xm_ref.shape[-1]
    x = _moe_combined(xm_ref, pmod_ref, pg_ref, (p0_ref, p1_ref, p2_ref, p3_ref))
    x_scr[...] = x
    mod = mod_ref[...]
    h = _rms_mod(x, g1_ref[...], mod[0:1], mod[1:2]).astype(MXU_DTYPE)
    width = wdw_ref.shape[0]
    half = (width - 1) // 2
    seg = GRID_W
    n_seg = TILE // seg
    stride = seg + 2 * CONV_PAD
    joined = jnp.where(tile == 0, 1.0, 0.0)
    zeros = jnp.zeros((CONV_PAD, d), F32)
    pad_scr[0:CONV_PAD, :] = zeros
    pad_scr[n_seg * stride - CONV_PAD:n_seg * stride, :] = zeros
    for n in range(0, d, N_CHUNK):
        a = _dot(h, w1_ref[:, n:n + N_CHUNK]) + b1_ref[:, n:n + N_CHUNK]
        gate = _dot(h, w1_ref[:, d + n:d + n + N_CHUNK]) + b1_ref[:, d + n:d + n + N_CHUNK]
        u = a * _sigmoid(gate)
        for s_i in range(n_seg):
            base = s_i * stride
            pad_scr[base + CONV_PAD:base + CONV_PAD + seg, n:n + N_CHUNK] = u[s_i * seg:(s_i + 1) * seg]
            if s_i > 0:
                pad_scr[base:base + CONV_PAD, n:n + N_CHUNK] = u[s_i * seg - CONV_PAD:s_i * seg] * joined
            if s_i < n_seg - 1:
                pad_scr[base + CONV_PAD + seg:base + stride, n:n + N_CHUNK] = (
                    u[(s_i + 1) * seg:(s_i + 1) * seg + CONV_PAD] * joined)
    lanes = 256
    pad_rows = n_seg * stride
    for n in range(0, d, lanes):
        blk = pad_scr[:, n:n + lanes]
        for s in range(1, SUBLANES):
            sh_scr[s - 1, :, n:n + lanes] = pltpu.roll(blk, pad_rows - s, axis=0)
    blocks = [range(s0, s0 + CV_SEGS) for s0 in range(0, n_seg, CV_SEGS)]
    x_new = []
    for segs in blocks:
        for s_i in segs:
            for n in range(0, d, lanes):
                acc = jnp.zeros((seg, lanes), F32) + bdw_ref[:, n:n + lanes]
                for k in range(width):
                    whole, phase = divmod(CONV_PAD + k - half, SUBLANES)
                    row0 = s_i * stride + whole * SUBLANES
                    if phase == 0:
                        src = pad_scr[row0:row0 + seg, n:n + lanes]
                    else:
                        src = sh_scr[phase - 1, row0:row0 + seg, n:n + lanes]
                    acc = acc + wdw_ref[k:k + 1, n:n + lanes] * src
                cv_scr[s_i * seg:(s_i + 1) * seg, n:n + lanes] = acc
        rs = slice(segs[0] * seg, (segs[-1] + 1) * seg)
        cv = cv_scr[rs, :]
        mu = jnp.mean(cv, axis=-1, keepdims=True)
        xc = cv - mu
        var = jnp.mean(xc * xc, axis=-1, keepdims=True)
        ln = xc * lax.rsqrt(var + EPS) * lng_ref[...] + lnb_ref[...]
        out = _dot(_silu(ln), w2_ref[...]) + b2_ref[...]
        x_new.append(x_scr[rs, :] + mod[2:3] * out)
    _moe_pre_init(cnt_scr)
    for segs, xb in zip(blocks, x_new):
        r0 = segs[0] * seg
        xo_ref[r0:r0 + xb.shape[0], :] = xb
        _moe_pre(xb, r0, mod, g2_ref[...], wr_ref[...], br_ref[...], h2_ref, topi_ref, gate_ref, rank_ref, cnt_ref,
                 cnt_scr)


def _conformer(pending, mod_i, g1, w_pw1, b_pw1, w_dw, b_dw, ln_g, ln_b, w_pw2, b_pw2, g2, w_router, b_router,
               with_ctx):
    x_mid, mod_prev, gates_t, picked = pending
    bsz, s, d = x_mid.shape
    t0 = 0 if with_ctx else 1
    nt_in = s // TILE
    nt = nt_in - t0
    ctx_row = bsz
    n_exp = w_router.shape[1]
    pad_rows = (TILE // GRID_W) * (GRID_W + 2 * CONV_PAD)
    pre_in, pre_out = _moe_pre_specs(d, n_exp)
    row = lambda v: v.astype(F32).reshape(1, -1)
    mod_spec = pl.BlockSpec((None, 6, d), lambda b, t: (jnp.where(t + t0 == 0, ctx_row, b), 0, 0))
    flat = pl.BlockSpec((TILE, d), lambda b, t: (b * nt_in + t + t0, 0))
    return pl.pallas_call(
        functools.partial(_cv_kernel, t0),
        grid=(bsz, nt),
        in_specs=[
            pl.BlockSpec((None, TILE, d), lambda b, t: (b, t + t0, 0)),
            mod_spec,
            pl.BlockSpec((None, TILE, TOP_K), lambda b, t: (b, t + t0, 0)),
            flat, flat, flat, flat,
            mod_spec,
            _full_spec((1, d)),
            _full_spec((d, 2 * d)), _full_spec((1, 2 * d)),
            _full_spec(w_dw.shape), _full_spec((1, d)), _full_spec((1, d)), _full_spec((1, d)),
            _full_spec((d, d)), _full_spec((1, d)),
        ] + pre_in,
        out_specs=pre_out,
        out_shape=_moe_pre_shapes(bsz, nt * TILE, d, n_exp),
        scratch_shapes=[
            pltpu.VMEM((pad_rows, d), F32),
            pltpu.VMEM((SUBLANES - 1, pad_rows, d), F32),
            pltpu.VMEM((TILE, d), F32),
            _moe_pre_scratch(n_exp),
            pltpu.VMEM((TILE, d), F32),
        ],
        compiler_params=_cparams(2, 56),
        name="conformer",
    )(x_mid, mod_prev, gates_t, *picked, mod_i,
      row(g1), w_pw1.astype(MXU_DTYPE), row(b_pw1), w_dw.astype(F32), row(b_dw), row(ln_g), row(ln_b),
      w_pw2.astype(MXU_DTYPE), row(b_pw2), row(g2), w_router.T.astype(F32),
      b_router.astype(F32).reshape(n_exp, 1))


def _moe_kernel(be_ref, flag_ref, x_ref, wgu_ref, bgu_ref, wdn_ref, bdn_ref, y_ref, wgu_scr, wdn_scr, act_scr):
    i = pl.program_id(0)
    flags = flag_ref[i]
    d_ff = wdn_ref.shape[0]

    @pl.when((flags & 2) != 0)
    def _():
        for n in range(0, 2 * d_ff, N_CHUNK):
            wgu_scr[:, n:n + N_CHUNK] = wgu_ref[:, n:n + N_CHUNK].astype(wgu_scr.dtype)
        for n in range(0, wdn_ref.shape[1], N_CHUNK):
            wdn_scr[:, n:n + N_CHUNK] = wdn_ref[:, n:n + N_CHUNK].astype(wdn_scr.dtype)

    @pl.when((flags & 1) != 0)
    def _():
        x = x_ref[...]
        for n in range(0, d_ff, N_CHUNK):
            gate = _dot(x, wgu_scr[:, n:n + N_CHUNK]) + bgu_ref[:, n:n + N_CHUNK]
            up = _dot(x, wgu_scr[:, d_ff + n:d_ff + n + N_CHUNK]) + bgu_ref[:, d_ff + n:d_ff + n + N_CHUNK]
            gate = jnp.minimum(gate, SWIGLU_LIMIT)
            up = jnp.clip(up, -SWIGLU_LIMIT, SWIGLU_LIMIT)
            glu = gate * _sigmoid(gate * SWIGLU_ALPHA)
            act_scr[:, n:n + N_CHUNK] = ((up + 1.0) * glu).astype(act_scr.dtype)
        act = act_scr[...]
        for n in range(0, y_ref.shape[1], N_CHUNK):
            y_ref[:, n:n + N_CHUNK] = (_dot(act, wdn_scr[:, n:n + N_CHUNK])
                                       + bdn_ref[:, n:n + N_CHUNK]).astype(y_ref.dtype)

    @pl.when((flags & 1) == 0)
    def _():
        y_ref[...] = jnp.zeros_like(y_ref)


def _moe_experts(xg, block_expert, block_flags, layer, w_gu, b_gu, w_down, b_down):
    n_rows, d = xg.shape
    depth, n_exp, _, d_gu = w_gu.shape
    d_ff = w_down.shape[2]
    nb = n_rows // MOE_TM
    grid_spec = pltpu.PrefetchScalarGridSpec(
        num_scalar_prefetch=2,
        grid=(nb,),
        in_specs=[
            pl.BlockSpec((MOE_TM, d), lambda i, be, fl: (i, 0)),
            pl.BlockSpec((None, None, d, d_gu), lambda i, be, fl: (layer, be[i], 0, 0)),
            pl.BlockSpec((None, None, 1, d_gu), lambda i, be, fl: (layer, be[i], 0, 0)),
            pl.BlockSpec((None, None, d_ff, d), lambda i, be, fl: (layer, be[i], 0, 0)),
            pl.BlockSpec((None, None, 1, d), lambda i, be, fl: (layer, be[i], 0, 0)),
        ],
        out_specs=pl.BlockSpec((MOE_TM, d), lambda i, be, fl: (i, 0)),
        scratch_shapes=[
            pltpu.VMEM((d, d_gu), MXU_DTYPE),
            pltpu.VMEM((d_ff, d), MXU_DTYPE),
            pltpu.VMEM((MOE_TM, d_ff), MXU_DTYPE),
        ],
    )
    return pl.pallas_call(
        _moe_kernel,
        grid_spec=grid_spec,
        out_shape=jax.ShapeDtypeStruct((n_rows, d), ACT_DTYPE),
        compiler_params=_cparams(1, 56),
        name="moe_experts",
    )(block_expert, block_flags, xg, w_gu, b_gu.reshape(depth, n_exp, 1, d_gu), w_down,
      b_down.reshape(depth, n_exp, 1, d))


def _take(rows, idx):
    return rows.at[idx].get(mode="promise_in_bounds")


def _route(topi, rank, counts):
    n_exp = counts.shape[0]
    n_assign = topi.size
    experts = jnp.arange(n_exp, dtype=jnp.int32)
    padded = (counts + MOE_TM - 1) // MOE_TM * MOE_TM
    pad_end = jnp.cumsum(padded)
    pad_start = pad_end - padded
    unpad_start = jnp.cumsum(counts) - counts
    start_of = jnp.sum(jnp.where(topi[..., None] == experts, pad_start, 0), axis=-1)
    slot_of_assign = start_of + rank
    nb = -(-n_assign // MOE_TM) + n_exp
    block_start = jnp.arange(nb, dtype=jnp.int32) * MOE_TM
    valid = block_start < pad_end[-1]
    be = jnp.minimum(jnp.sum((block_start[:, None] >= pad_end[None, :]).astype(jnp.int32), axis=1), n_exp - 1)
    n_valid = pad_end[-1] // MOE_TM
    be = jnp.where(valid, be, be[jnp.maximum(n_valid - 1, 0)])
    first = jnp.logical_and(valid, block_start == pad_start[be])
    flags = valid.astype(jnp.int32) + 2 * first.astype(jnp.int32)
    by_slot = jnp.argsort(slot_of_assign.reshape(-1)).astype(jnp.int32)
    n_tok = topi.shape[1]
    slot = jnp.arange(nb * MOE_TM, dtype=jnp.int32)
    per_slot = lambda per_block: jnp.repeat(per_block, MOE_TM)
    r_slot = slot - per_slot(pad_start[be])
    occupied = jnp.logical_and(per_slot(valid), r_slot < per_slot(counts[be]))
    src = jnp.clip(per_slot(unpad_start[be]) + r_slot, 0, n_assign - 1)
    tok_of_slot = jnp.where(occupied, _take(by_slot, src) % n_tok, slot % n_tok)
    return tok_of_slot, slot_of_assign, be, flags


def _combine_kernel(final, x_ref, mod_ref, g_ref, p0_ref, p1_ref, p2_ref, p3_ref, fg_ref, o_ref):
    x = _moe_combined(x_ref, mod_ref, g_ref, (p0_ref, p1_ref, p2_ref, p3_ref))
    if final:
        ms = jnp.mean(x * x, axis=-1, keepdims=True)
        x = x * lax.rsqrt(ms + EPS) * fg_ref[...]
    o_ref[...] = x


def _combine(pending, final_g, with_ctx, final):
    x_mid, mod_i, gates_t, picked = pending
    bsz, s, d = x_mid.shape
    nt = s // TILE
    ctx_row = bsz
    mod_row = (lambda b, t: jnp.where(t == 0, ctx_row, b)) if with_ctx else (lambda b, t: b)
    tok = pl.BlockSpec((None, TILE, d), lambda b, t: (b, t, 0))
    flat = pl.BlockSpec((TILE, d), lambda b, t: (b * nt + t, 0))
    return pl.pallas_call(
        functools.partial(_combine_kernel, final),
        grid=(bsz, nt),
        in_specs=[
            tok,
            pl.BlockSpec((None, 6, d), lambda b, t: (mod_row(b, t), 0, 0)),
            pl.BlockSpec((None, TILE, TOP_K), lambda b, t: (b, t, 0)),
            flat, flat, flat, flat,
            _full_spec((1, d)),
        ],
        out_specs=tok,
        out_shape=jax.ShapeDtypeStruct((bsz, s, d), F32),
        compiler_params=_cparams(2, 32),
        name="moe_combine",
    )(x_mid, mod_i, gates_t, *picked, final_g.astype(F32).reshape(1, d))


def _moe(x_mid, mod_i, h2, topi, gates, rank, counts, layer, w_gu, b_gu, w_down, b_down):
    bsz, s, d = h2.shape
    flat = lambda v: v.transpose(1, 0, 2).reshape(TOP_K, bsz * s)
    tok_of_slot, slot_of_assign, be, flags = _route(flat(topi), flat(rank), counts[:, 0].astype(jnp.int32))
    xg = _take(h2.reshape(bsz * s, d), tok_of_slot)
    y = _moe_experts(xg, be, flags, layer, w_gu, b_gu, w_down, b_down)
    picked = [_take(y, slot_of_assign[k]) for k in range(TOP_K)]
    return x_mid, mod_i, gates.transpose(0, 2, 1), picked


def kernel(x, c, ctx, c_ctx, w_ada, b_ada, norm1_g, norm2_g, ssd_w_in, ssd_w_conv, ssd_b_conv, ssd_dt_bias, ssd_a_log, ssd_d, ssd_norm_g, ssd_w_out, cv_w_pw1, cv_b_pw1, cv_w_dw, cv_b_dw, cv_ln_g, cv_ln_b, cv_w_pw2, cv_b_pw2, moe_w_router, moe_b_router, moe_w_gu, moe_b_gu, moe_w_down, moe_b_down, final_g):
    bsz, seq, d = x.shape
    n_ctx = ctx.shape[1]
    depth = w_ada.shape[0]
    assert n_ctx == TILE and seq % TILE == 0 and TILE % GRID_W == 0 and TILE % SSD_CHUNK == 0
    assert depth % N_MIXERS == 0

    mod_rows = -(-(bsz + 1) // 8) * 8
    c_all = jnp.zeros((mod_rows, d), F32).at[:bsz].set(c).at[bsz].set(c_ctx)
    mods = _ada_mods(c_all, w_ada, b_ada)
    xs = jnp.concatenate([ctx, x], axis=1)

    pending = None
    for i in range(depth):
        j = i // N_MIXERS
        use_ssd = (i % N_MIXERS) == 0
        need_ctx = i < depth - 1
        mod_i = mods[i]
        if use_ssd:
            if pending is not None:
                xs = _combine(pending, final_g, True, False)
            z_g, xbc, acum, acum_t = _ssd_inproj(xs, mod_i, norm1_g[i], ssd_w_in[j], ssd_dt_bias[j], ssd_a_log[j])
            xs_g, bmt_g, cm_g = _ssd_conv(xbc, ssd_w_conv[j], ssd_b_conv[j])
            y = _ssd_scan(xs_g, bmt_g, cm_g, acum, acum_t, n_ctx // SSD_CHUNK)
            x_mid, h2, topi, gates, rank, counts = _ssd_out(
                xs, mod_i, y, xs_g, z_g, ssd_d[j], ssd_norm_g[j], ssd_w_out[j], norm2_g[i], moe_w_router[i],
                moe_b_router[i])
        else:
            x_mid, h2, topi, gates, rank, counts = _conformer(
                pending, mod_i, norm1_g[i], cv_w_pw1[j], cv_b_pw1[j], cv_w_dw[j], cv_b_dw[j], cv_ln_g[j],
                cv_ln_b[j], cv_w_pw2[j], cv_b_pw2[j], norm2_g[i], moe_w_router[i], moe_b_router[i], need_ctx)
        pending = _moe(x_mid, mod_i, h2, topi, gates, rank, counts, i, moe_w_gu, moe_b_gu, moe_w_down, moe_b_down)
    return _combine(pending, final_g, False, True)
```
